```python
import numpy as np
import jax
import jax.numpy as jnp
from jax import lax


D_MODEL = 1024
BATCH = 2
SEQ = 16384
DEPTH = 2
DEC_BATCH = 32
DEC_SEQ = 2048
PAST_LEN = 128

HEAD_DIM = 64
ATTN_SCALE = HEAD_DIM ** -0.5
A_HEADS = D_MODEL // 2 // HEAD_DIM
A_WIDTH = A_HEADS * HEAD_DIM
A_BRANCHES = ((128, 1), (512, 4), (2048, 16))
A_BLOCK = 128
B_GROUPS = 4
B_WINDOWS = (2, 4, 8, 16)
B_WIDTH = D_MODEL // 2
B_GROUP_DIM = B_WIDTH // B_GROUPS
EVEN_IN = 3 * A_WIDTH + B_WIDTH
EVEN_MIX = A_WIDTH + B_WIDTH
C_HEADS = D_MODEL // HEAD_DIM
C_WIDTH = C_HEADS * HEAD_DIM
GRID_W = 64
NA_ROWS = 8
NA_COLS = 16
T5_BUCKETS = 32
T5_MAX_DIST = 1024
D_FF = -(-8 * D_MODEL // (3 * 256)) * 256
N_EVEN = (DEPTH + 1) // 2
N_ODD = DEPTH // 2
NEG = -1e30
EPS = 1e-6

kernel_name = 'hybrid_dilated_pool_natten_encoder'


def rms_norm(x, g):
    xf = x.astype(jnp.float32)
    y = xf * lax.rsqrt(jnp.mean(xf * xf, axis=-1, keepdims=True) + EPS)
    return (y * g.astype(jnp.float32)).astype(x.dtype)


def modulate(h, shift, scale):
    return h * (1 + scale[:, None, :]) + shift[:, None, :]


def t5_bucket(rel):
    nb = T5_BUCKETS // 2
    max_exact = nb // 2
    ret = (rel > 0).astype(np.int32) * nb
    n = np.abs(rel)
    large = max_exact + (np.log(np.maximum(n, 1) / max_exact) / np.log(T5_MAX_DIST / max_exact)
                         * (nb - max_exact)).astype(np.int32)
    large = np.minimum(large, nb - 1)
    return (ret + np.where(n < max_exact, n, large)).astype(np.int32)


def dilated_branch(q, k, v, t5_table, window, dilation):
    b, L, h, dh = q.shape
    r = (window // 2) // dilation
    Ls = L // dilation

    def to_sub(t):
        return t.reshape(b, Ls, dilation, h, dh).transpose(0, 2, 1, 3, 4).reshape(b * dilation, Ls, h, dh)

    qs, ks, vs = to_sub(q), to_sub(k), to_sub(v)
    bq = min(A_BLOCK, Ls)
    nb = -(-Ls // bq)
    Lp = nb * bq
    kw = bq + 2 * r
    qs = jnp.pad(qs, ((0, 0), (0, Lp - Ls), (0, 0), (0, 0)))
    pad_k = ((0, 0), (r, Lp - Ls + r), (0, 0), (0, 0))
    ks = jnp.pad(ks, pad_k)
    vs = jnp.pad(vs, pad_k)
    idx = np.arange(nb)[:, None] * bq + np.arange(kw)[None, :]
    kb = ks[:, idx]
    vb = vs[:, idx]
    qb = qs.reshape(-1, nb, bq, h, dh)
    rel = np.arange(kw)[None, :] - r - np.arange(bq)[:, None]
    kpos = idx - r
    valid = (np.abs(rel) <= r)[None] & ((kpos >= 0) & (kpos < Ls))[:, None, :]
    bias = jnp.transpose(t5_table[t5_bucket(rel * dilation)], (2, 0, 1)).astype(jnp.float32)
    s = jnp.einsum('nbqhd,nbkhd->nbhqk', qb, kb, preferred_element_type=jnp.float32) * ATTN_SCALE + bias
    s = jnp.where(valid[None, :, None], s, NEG)
    m = jnp.max(s, axis=-1)
    p = jnp.exp(s - m[..., None])
    den = jnp.sum(p, axis=-1)
    o = jnp.einsum('nbhqk,nbkhd->nbqhd', p.astype(vb.dtype), vb, preferred_element_type=jnp.float32)
    o = o / jnp.transpose(den, (0, 1, 3, 2))[..., None]
    lse = jnp.transpose(m + jnp.log(den), (0, 1, 3, 2))
    o = o.reshape(b, dilation, Lp, h, dh)[:, :, :Ls].transpose(0, 2, 1, 3, 4).reshape(b, L, h, dh)
    lse = lse.reshape(b, dilation, Lp, h)[:, :, :Ls].transpose(0, 2, 1, 3).reshape(b, L, h)
    return o, lse


def multiscale_pool(u, w_grp, scale):
    b, L, _ = u.shape
    ug = u.reshape(b, L, B_GROUPS, B_GROUP_DIM)
    t = np.arange(L)
    outs = []
    for g, w in enumerate(B_WINDOWS):
        ch = ug[:, :, g].astype(jnp.float32)
        cs = jnp.concatenate([jnp.zeros((b, 1, B_GROUP_DIM), jnp.float32), jnp.cumsum(ch, axis=1)], axis=1)
        lo = np.clip(t - w // 2, 0, L)
        hi = np.clip(t + w // 2, 0, L)
        cnt = (hi - lo).astype(np.float32)
        mean = (cs[:, hi] - cs[:, lo]) / cnt[None, :, None]
        outs.append(mean - ch)
    pooled = jnp.stack(outs, axis=2)
    y = jnp.einsum('blgc,gcd->blgd', pooled, w_grp.astype(jnp.float32))
    y = y * scale.astype(jnp.float32).reshape(B_GROUPS, B_GROUP_DIM)
    return y.reshape(b, L, B_WIDTH).astype(u.dtype)


def even_mixer(h, w_in, w_out, pool_w, pool_scale, t5_table):
    b, L, _ = h.shape
    z = h @ w_in
    q, k, v, u = jnp.split(z, [A_WIDTH, 2 * A_WIDTH, 3 * A_WIDTH], axis=-1)
    q = q.reshape(b, L, A_HEADS, HEAD_DIM)
    k = k.reshape(b, L, A_HEADS, HEAD_DIM)
    v = v.reshape(b, L, A_HEADS, HEAD_DIM)
    outs, lses = [], []
    for window, dilation in A_BRANCHES:
        o, l = dilated_branch(q, k, v, t5_table, window, dilation)
        outs.append(o)
        lses.append(l)
    wts = jax.nn.softmax(jnp.stack(lses, axis=0), axis=0)
    ya = jnp.sum(wts[..., None] * jnp.stack(outs, axis=0), axis=0)
    ya = ya.reshape(b, L, A_WIDTH).astype(h.dtype)
    yb = multiscale_pool(u, pool_w, pool_scale)
    return jnp.concatenate([ya, yb], axis=-1) @ w_out


def neighbourhood_attn(q, k, v, rpb):
    b, L, h, dh = q.shape
    rows = L // GRID_W
    kh = min(NA_ROWS, rows)
    kw = NA_COLS
    qg = q.reshape(b, rows, GRID_W, h, dh)
    kg = k.reshape(b, rows, GRID_W, h, dh)
    vg = v.reshape(b, rows, GRID_W, h, dh)
    c = np.arange(GRID_W)
    cstart = np.clip(c - kw // 2, 0, GRID_W - kw)
    colmask = (c[None, :] >= cstart[:, None]) & (c[None, :] < cstart[:, None] + kw)
    coloff = np.clip(c[None, :] - c[:, None] + NA_COLS - 1, 0, 2 * NA_COLS - 2)

    def row_block(r):
        rs = jnp.clip(r - kh // 2, 0, rows - kh)
        qr = lax.dynamic_index_in_dim(qg, r, axis=1, keepdims=False)
        kr = lax.dynamic_slice_in_dim(kg, rs, kh, axis=1)
        vr = lax.dynamic_slice_in_dim(vg, rs, kh, axis=1)
        rowoff = rs + jnp.arange(kh) - r + (NA_ROWS - 1)
        bias = jnp.take(rpb, rowoff, axis=1)[:, :, coloff]
        bias = jnp.transpose(bias, (0, 2, 1, 3)).astype(jnp.float32)
        s = jnp.einsum('bqhd,bikhd->bhqik', qr, kr, preferred_element_type=jnp.float32) * ATTN_SCALE + bias
        s = jnp.where(colmask[:, None, :], s, NEG)
        p = jax.nn.softmax(s.reshape(b, h, GRID_W, kh * GRID_W), axis=-1)
        return jnp.einsum('bhqn,bnhd->bqhd', p.astype(vr.dtype), vr.reshape(b, kh * GRID_W, h, dh))

    out = lax.map(row_block, jnp.arange(rows))
    return jnp.transpose(out, (1, 0, 2, 3, 4)).reshape(b, L, h * dh)


def odd_mixer(h, w_qkv, w_out, rpb):
    b, L, _ = h.shape
    q, k, v = jnp.split(h @ w_qkv, 3, axis=-1)
    q = q.reshape(b, L, C_HEADS, HEAD_DIM)
    k = k.reshape(b, L, C_HEADS, HEAD_DIM)
    v = v.reshape(b, L, C_HEADS, HEAD_DIM)
    return neighbourhood_attn(q, k, v, rpb).astype(h.dtype) @ w_out


def swiglu(h, w_gate, w_up, w_down):
    return (jax.nn.silu(h @ w_gate) * (h @ w_up)) @ w_down


def setup_inputs(seed: int = 0) -> dict:
    key = jax.random.key(seed)
    ks = jax.random.split(key, 20)
    f32 = jnp.float32

    def nrm(k, shape, s):
        return jax.random.normal(k, shape, f32) * s

    return {
        'x_prompt': nrm(ks[0], (BATCH, SEQ, D_MODEL), 1.0),
        'x_sample': nrm(ks[1], (DEC_BATCH, DEC_SEQ, D_MODEL), 1.0),
        'c_prompt': nrm(ks[2], (BATCH, D_MODEL), 1.0),
        'c_sample': nrm(ks[3], (DEC_BATCH, D_MODEL), 1.0),
        'norm_g': 1.0 + nrm(ks[4], (DEPTH, 4, D_MODEL), 0.05),
        'ada_w': nrm(ks[5], (DEPTH, D_MODEL, 6 * D_MODEL), 0.5 * D_MODEL ** -0.5),
        'ada_b': nrm(ks[6], (DEPTH, 6 * D_MODEL), 0.02),
        'ffn_w_gate': nrm(ks[7], (DEPTH, D_MODEL, D_FF), D_MODEL ** -0.5),
        'ffn_w_up': nrm(ks[8], (DEPTH, D_MODEL, D_FF), D_MODEL ** -0.5),
        'ffn_w_down': nrm(ks[9], (DEPTH, D_FF, D_MODEL), D_FF ** -0.5),
        'even_w_in': nrm(ks[10], (N_EVEN, D_MODEL, EVEN_IN), D_MODEL ** -0.5),
        'even_w_out': nrm(ks[11], (N_EVEN, EVEN_MIX, D_MODEL), EVEN_MIX ** -0.5),
        'pool_w': nrm(ks[12], (N_EVEN, B_GROUPS, B_GROUP_DIM, B_GROUP_DIM), B_GROUP_DIM ** -0.5),
        'pool_scale': 1.0 + nrm(ks[13], (N_EVEN, B_WIDTH), 0.1),
        't5_table': nrm(ks[14], (T5_BUCKETS, A_HEADS), 0.5),
        'odd_w_qkv': nrm(ks[15], (N_ODD, D_MODEL, 3 * C_WIDTH), D_MODEL ** -0.5),
        'odd_w_out': nrm(ks[16], (N_ODD, C_WIDTH, D_MODEL), C_WIDTH ** -0.5),
        'odd_rpb': nrm(ks[17], (N_ODD, C_HEADS, 2 * NA_ROWS - 1, 2 * NA_COLS - 1), 0.5),
    }


def reference(x_prompt, x_sample, c_prompt, c_sample, norm_g, ada_w, ada_b, ffn_w_gate, ffn_w_up,
              ffn_w_down, even_w_in, even_w_out, pool_w, pool_scale, t5_table, odd_w_qkv, odd_w_out, odd_rpb):
    def trunk(x, c):
        for layer in range(DEPTH):
            mod = jax.nn.silu(c) @ ada_w[layer] + ada_b[layer]
            sh1, sc1, g1, sh2, sc2, g2 = jnp.split(mod, 6, axis=-1)
            g = norm_g[layer]
            h = modulate(rms_norm(x, g[0]), sh1, sc1)
            i = layer // 2
            if layer % 2 == 0:
                y = even_mixer(h, even_w_in[i], even_w_out[i], pool_w[i], pool_scale[i], t5_table)
            else:
                y = odd_mixer(h, odd_w_qkv[i], odd_w_out[i], odd_rpb[i])
            x = x + g1[:, None, :] * rms_norm(y, g[1])
            h = modulate(rms_norm(x, g[2]), sh2, sc2)
            y = swiglu(h, ffn_w_gate[layer], ffn_w_up[layer], ffn_w_down[layer])
            x = x + g2[:, None, :] * rms_norm(y, g[3])
        return x

    y_prompt = trunk(x_prompt, c_prompt)
    y_sample = trunk(x_sample, c_sample)
    return (y_prompt, y_sample)
```

```python
import functools

import numpy as np
import jax
import jax.numpy as jnp
from jax import lax
from jax.experimental import pallas as pl
from jax.experimental.pallas import tpu as pltpu

D_MODEL = 1024
HEAD_DIM = 64
ATTN_SCALE = HEAD_DIM ** -0.5
A_HEADS = 8
A_WIDTH = A_HEADS * HEAD_DIM
A_BRANCHES = ((128, 1), (512, 4), (2048, 16))
A_BLOCK = 128
A_RADIUS = 64
A_CHUNK = 512
B_GROUPS = 4
B_WINDOWS = (2, 4, 8, 16)
B_WIDTH = 512
B_GROUP_DIM = 128
B_HALO = 16
EVEN_IN = 3 * A_WIDTH + B_WIDTH
C_HEADS = 16
C_WIDTH = C_HEADS * HEAD_DIM
GRID_W = 64
NA_ROWS = 8
NA_COLS = 16
NA_CHUNK_ROWS = 8
T5_BUCKETS = 32
T5_MAX_DIST = 1024
NEG = -1e30
EPS = 1e-6
LSE_LANES = 128
V7X_VMEM_LIMIT_BYTES = 56 * 1024 * 1024

BF16 = jnp.bfloat16
F32 = jnp.float32


def _const_spec(shape):
    zeros = (0,) * len(shape)
    return pl.BlockSpec(shape, lambda *_: zeros, pipeline_mode=pl.Buffered(1))


def _params(n_axes):
    return pltpu.CompilerParams(dimension_semantics=("arbitrary",) * n_axes,
                                vmem_limit_bytes=V7X_VMEM_LIMIT_BYTES)


def _rms(x, g):
    return x * lax.rsqrt(jnp.mean(x * x, axis=-1, keepdims=True) + EPS) * g


def _ada_kernel(c_ref, w_ref, b_ref, o_ref):
    c = c_ref[...]
    s = (c * jax.nn.sigmoid(c)).astype(BF16)
    o_ref[...] = jnp.dot(s, w_ref[...].astype(BF16), preferred_element_type=F32) + b_ref[...]


def _ada(c_all, ada_w, ada_b):
    depth, d, six_d = ada_w.shape
    nb = c_all.shape[0]
    return pl.pallas_call(
        _ada_kernel,
        grid=(depth, six_d // d),
        in_specs=[pl.BlockSpec((nb, d), lambda l, j: (0, 0)),
                  pl.BlockSpec((None, d, d), lambda l, j: (l, 0, j)),
                  pl.BlockSpec((None, 1, d), lambda l, j: (l, 0, j))],
        out_specs=pl.BlockSpec((None, nb, d), lambda l, j: (l, 0, j)),
        out_shape=jax.ShapeDtypeStruct((depth, nb, six_d), F32),
        compiler_params=_params(2),
        name="ada_mod",
    )(c_all, ada_w, ada_b.reshape(depth, 1, six_d))


def _pre_kernel(x_ref, g_ref, sh_ref, sc_ref, w_ref, o_ref, *, col_chunk):
    h = _rms(x_ref[...], g_ref[...]) * (1.0 + sc_ref[...]) + sh_ref[...]
    hb = h.astype(BF16)
    for j in range(w_ref.shape[1] // col_chunk):
        cols = slice(j * col_chunk, (j + 1) * col_chunk)
        o_ref[:, cols] = jnp.dot(hb, w_ref[:, cols], preferred_element_type=F32).astype(BF16)


def _pre(x, g, shift, scale, w, tm):
    b, l, d = x.shape
    n = w.shape[1]
    vec = pl.BlockSpec((None, 1, d), lambda i, t: (i, 0, 0))
    return pl.pallas_call(
        functools.partial(_pre_kernel, col_chunk=1024),
        grid=(b, l // tm),
        in_specs=[pl.BlockSpec((None, tm, d), lambda i, t: (i, t, 0)),
                  _const_spec((1, d)), vec, vec, _const_spec((d, n))],
        out_specs=pl.BlockSpec((None, tm, n), lambda i, t: (i, t, 0)),
        out_shape=jax.ShapeDtypeStruct((b, l, n), BF16),
        compiler_params=_params(2),
        name="pre_proj",
    )(x, g.reshape(1, d), shift, scale, w)


def _t5_bucket(rel):
    nb = T5_BUCKETS // 2
    max_exact = nb // 2
    ret = (rel > 0).astype(np.int32) * nb
    n = np.abs(rel)
    large = max_exact + (np.log(np.maximum(n, 1) / max_exact) / np.log(T5_MAX_DIST / max_exact)
                         * (nb - max_exact)).astype(np.int32)
    large = np.minimum(large, nb - 1)
    return (ret + np.where(n < max_exact, n, large)).astype(np.int32)


def _dilated_bias(t5_table, dilation):
    kw = A_BLOCK + 2 * A_RADIUS
    rel = np.arange(kw)[None, :] - A_RADIUS - np.arange(A_BLOCK)[:, None]
    bias = jnp.transpose(t5_table[_t5_bucket(rel * dilation)], (2, 0, 1)).astype(F32)
    return jnp.where(np.abs(rel)[None] <= A_RADIUS, bias, NEG)


def _dilated_kernel(q_ref, kvp_ref, kvc_ref, kvn_ref, bias_ref, o_ref, lse_ref, kv_buf, *, chunk, sub_len):
    kw = A_BLOCK + 2 * A_RADIUS
    kv_buf[0:A_RADIUS] = kvp_ref[...]
    kv_buf[A_RADIUS:A_RADIUS + chunk] = kvc_ref[...]
    kv_buf[A_RADIUS + chunk:] = kvn_ref[...]
    chunk_start = pl.program_id(2) * chunk
    lane_head = lax.broadcasted_iota(jnp.int32, (A_BLOCK, LSE_LANES), 1) // (LSE_LANES // A_HEADS)

    def block(j, carry):
        row0 = pl.multiple_of(j * A_BLOCK, A_BLOCK)
        kpos = chunk_start + row0 - A_RADIUS + lax.broadcasted_iota(jnp.int32, (1, kw), 1)
        valid = (kpos >= 0) & (kpos < sub_len)
        lse_tile = jnp.zeros((A_BLOCK, LSE_LANES), F32)
        outs = []
        for h in range(A_HEADS):
            cols = slice(h * HEAD_DIM, (h + 1) * HEAD_DIM)
            vcols = slice(A_WIDTH + h * HEAD_DIM, A_WIDTH + (h + 1) * HEAD_DIM)
            q = q_ref[pl.ds(row0, A_BLOCK), cols]
            k = kv_buf[pl.ds(row0, kw), cols]
            v = kv_buf[pl.ds(row0, kw), vcols]
            s = lax.dot_general(q, k, (((1,), (1,)), ((), ())), preferred_element_type=F32)
            s = jnp.where(valid, s + bias_ref[h], NEG)
            m = jnp.max(s, axis=-1, keepdims=True)
            p = jnp.exp(s - m)
            den = jnp.sum(p, axis=-1, keepdims=True)
            o = jnp.dot(p.astype(BF16), v, preferred_element_type=F32) / den
            outs.append(o)
            lse_tile = jnp.where(lane_head == h, m + jnp.log(den), lse_tile)
        o_ref[pl.ds(row0, A_BLOCK), :] = jnp.concatenate(outs, axis=-1).astype(BF16)
        lse_ref[pl.ds(row0, A_BLOCK), :] = lse_tile
        return carry

    lax.fori_loop(0, chunk // A_BLOCK, block, 0)


def _dilated_branch(z, bias, dilation):
    b, l, n = z.shape
    sub_len = l // dilation
    chunk = min(A_CHUNK, sub_len)
    halo_blocks = sub_len // A_RADIUS
    per_chunk = chunk // A_RADIUS
    zv = z.reshape(b, sub_len, dilation * n)
    kw = A_BLOCK + 2 * A_RADIUS
    kv_w = 2 * A_WIDTH
    o, lse = pl.pallas_call(
        functools.partial(_dilated_kernel, chunk=chunk, sub_len=sub_len),
        grid=(b, dilation, sub_len // chunk),
        in_specs=[
            pl.BlockSpec((None, chunk, A_WIDTH), lambda i, r, c: (i, c, r * (n // A_WIDTH))),
            pl.BlockSpec((None, A_RADIUS, kv_w),
                         lambda i, r, c: (i, jnp.maximum(c * per_chunk - 1, 0), r * (n // kv_w) + 1)),
            pl.BlockSpec((None, chunk, kv_w), lambda i, r, c: (i, c, r * (n // kv_w) + 1)),
            pl.BlockSpec((None, A_RADIUS, kv_w),
                         lambda i, r, c: (i, jnp.minimum((c + 1) * per_chunk, halo_blocks - 1), r * (n // kv_w) + 1)),
            _const_spec((A_HEADS, A_BLOCK, kw)),
        ],
        out_specs=[pl.BlockSpec((None, chunk, A_WIDTH), lambda i, r, c: (i, c, r)),
                   pl.BlockSpec((None, chunk, LSE_LANES), lambda i, r, c: (i, c, r))],
        out_shape=[jax.ShapeDtypeStruct((b, sub_len, dilation * A_WIDTH), BF16),
                   jax.ShapeDtypeStruct((b, sub_len, dilation * LSE_LANES), F32)],
        scratch_shapes=[pltpu.VMEM((chunk + 2 * A_RADIUS, kv_w), BF16)],
        compiler_params=_params(3),
        name=f"dilated_d{dilation}",
    )(zv, zv, zv, zv, bias)
    return o.reshape(b, l, A_WIDTH), lse.reshape(b, l, LSE_LANES)


def _natten_bias(rpb):
    c = np.arange(GRID_W)
    cstart = np.clip(c - NA_COLS // 2, 0, GRID_W - NA_COLS)
    colmask = (c[None, :] >= cstart[:, None]) & (c[None, :] < cstart[:, None] + NA_COLS)
    coloff = np.clip(c[None, :] - c[:, None] + NA_COLS - 1, 0, 2 * NA_COLS - 2)
    tables = []
    for delta in range(NA_ROWS):
        rowoff = np.arange(NA_ROWS) - delta + (NA_ROWS - 1)
        bias = jnp.take(rpb, rowoff, axis=1)[:, :, coloff].astype(F32)
        bias = jnp.where(colmask[None, None], bias, NEG)
        tables.append(jnp.transpose(bias, (0, 2, 1, 3)).reshape(C_HEADS, GRID_W, NA_ROWS * GRID_W))
    return jnp.stack(tables, axis=0)


def _natten_kernel(q_ref, kp_ref, kc_ref, kn_ref, vp_ref, vc_ref, vn_ref, bias_ref, o_ref, k_buf, v_buf,
                   *, grid_rows):
    halo = (NA_ROWS // 2) * GRID_W
    chunk = NA_CHUNK_ROWS * GRID_W
    nkeys = NA_ROWS * GRID_W
    k_buf[0:halo] = kp_ref[...]
    k_buf[halo:halo + chunk] = kc_ref[...]
    k_buf[halo + chunk:] = kn_ref[...]
    v_buf[0:halo] = vp_ref[...]
    v_buf[halo:halo + chunk] = vc_ref[...]
    v_buf[halo + chunk:] = vn_ref[...]
    row_base = pl.program_id(1) * NA_CHUNK_ROWS

    def query_row(rr, carry):
        r = row_base + rr
        rs = jnp.clip(r - NA_ROWS // 2, 0, grid_rows - NA_ROWS)
        delta = r - rs
        q0 = pl.multiple_of(rr * GRID_W, GRID_W)
        k0 = pl.multiple_of((rs - (row_base - NA_ROWS // 2)) * GRID_W, GRID_W)
        outs = []
        for h in range(C_HEADS):
            cols = slice(h * HEAD_DIM, (h + 1) * HEAD_DIM)
            q = q_ref[pl.ds(q0, GRID_W), cols]
            k = k_buf[pl.ds(k0, nkeys), cols]
            v = v_buf[pl.ds(k0, nkeys), cols]
            s = lax.dot_general(q, k, (((1,), (1,)), ((), ())), preferred_element_type=F32)
            s = s + bias_ref[delta, h]
            m = jnp.max(s, axis=-1, keepdims=True)
            p = jnp.exp(s - m)
            den = jnp.sum(p, axis=-1, keepdims=True)
            outs.append(jnp.dot(p.astype(BF16), v, preferred_element_type=F32) / den)
        o_ref[pl.ds(q0, GRID_W), :] = jnp.concatenate(outs, axis=-1).astype(BF16)
        return carry

    lax.fori_loop(0, NA_CHUNK_ROWS, query_row, 0)


def _natten(z, bias):
    b, l, _ = z.shape
    grid_rows = l // GRID_W
    chunk = NA_CHUNK_ROWS * GRID_W
    halo = (NA_ROWS // 2) * GRID_W
    halo_blocks = l // halo
    per_chunk = chunk // halo

    def cur(col):
        return pl.BlockSpec((None, chunk, C_WIDTH), lambda i, c: (i, c, col))

    def prev(col):
        return pl.BlockSpec((None, halo, C_WIDTH), lambda i, c: (i, jnp.maximum(c * per_chunk - 1, 0), col))

    def nxt(col):
        return pl.BlockSpec((None, halo, C_WIDTH),
                            lambda i, c: (i, jnp.minimum((c + 1) * per_chunk, halo_blocks - 1), col))

    return pl.pallas_call(
        functools.partial(_natten_kernel, grid_rows=grid_rows),
        grid=(b, l // chunk),
        in_specs=[cur(0), prev(1), cur(1), nxt(1), prev(2), cur(2), nxt(2),
                  _const_spec((NA_ROWS, C_HEADS, GRID_W, NA_ROWS * GRID_W))],
        out_specs=pl.BlockSpec((None, chunk, C_WIDTH), lambda i, c: (i, c, 0)),
        out_shape=jax.ShapeDtypeStruct((b, l, C_WIDTH), BF16),
        scratch_shapes=[pltpu.VMEM((chunk + 2 * halo, C_WIDTH), BF16),
                        pltpu.VMEM((chunk + 2 * halo, C_WIDTH), BF16)],
        compiler_params=_params(2),
        name="natten",
    )(z, z, z, z, z, z, z, bias)


def _ffn_tail(x, y, g_ref, gate1_ref, sh2_ref, sc2_ref, gate2_ref, wg_ref, wu_ref, wd_ref, o_ref):
    x1 = x + gate1_ref[...] * _rms(y, g_ref[0:1])
    h = (_rms(x1, g_ref[1:2]) * (1.0 + sc2_ref[...]) + sh2_ref[...]).astype(BF16)
    gate = jnp.dot(h, wg_ref[...], preferred_element_type=F32)
    up = jnp.dot(h, wu_ref[...], preferred_element_type=F32)
    act = (gate * jax.nn.sigmoid(gate) * up).astype(BF16)
    y2 = jnp.dot(act, wd_ref[...], preferred_element_type=F32)
    o_ref[...] = x1 + gate2_ref[...] * _rms(y2, g_ref[2:3])


def _post_even_kernel(x_ref, o1_ref, o2_ref, o3_ref, l1_ref, l2_ref, l3_ref, up_ref, uc_ref, un_ref,
                      expand_ref, pw_ref, ps_ref, wo_ref, g_ref, gate1_ref, sh2_ref, sc2_ref, gate2_ref,
                      wg_ref, wu_ref, wd_ref, o_ref, u_buf, *, tm, seq_len):
    l1, l2, l3 = l1_ref[...], l2_ref[...], l3_ref[...]
    mx = jnp.maximum(jnp.maximum(l1, l2), l3)
    e1, e2, e3 = jnp.exp(l1 - mx), jnp.exp(l2 - mx), jnp.exp(l3 - mx)
    inv = 1.0 / (e1 + e2 + e3)
    ya = jnp.zeros((tm, A_WIDTH), F32)
    for e, o_br in ((e1, o1_ref), (e2, o2_ref), (e3, o3_ref)):
        wts = jnp.dot((e * inv).astype(BF16), expand_ref[...], preferred_element_type=F32)
        ya = ya + wts * o_br[...].astype(F32)

    t0 = pl.program_id(1) * tm
    u_buf[0:B_HALO] = jnp.where(t0 > 0, up_ref[...].astype(F32), 0.0)
    u_buf[B_HALO:B_HALO + tm] = uc_ref[...].astype(F32)
    u_buf[B_HALO + tm:] = jnp.where(t0 + tm < seq_len, un_ref[...].astype(F32), 0.0)
    t = t0 + lax.broadcasted_iota(jnp.int32, (tm, 1), 0)
    yb = []
    for g, w in enumerate(B_WINDOWS):
        cols = slice(g * B_GROUP_DIM, (g + 1) * B_GROUP_DIM)
        total = jnp.zeros((tm, B_GROUP_DIM), F32)
        for j in range(-(w // 2), w // 2):
            total = total + u_buf[B_HALO + j:B_HALO + j + tm, cols]
        cnt = (jnp.clip(t + w // 2, 0, seq_len) - jnp.clip(t - w // 2, 0, seq_len)).astype(F32)
        pooled = total / cnt - u_buf[B_HALO:B_HALO + tm, cols]
        yg = jnp.dot(pooled.astype(BF16), pw_ref[g], preferred_element_type=F32)
        yb.append(yg * ps_ref[:, cols])
    mix = jnp.concatenate([ya] + yb, axis=-1).astype(BF16)
    y = jnp.dot(mix, wo_ref[...], preferred_element_type=F32)
    _ffn_tail(x_ref[...], y, g_ref, gate1_ref, sh2_ref, sc2_ref, gate2_ref, wg_ref, wu_ref, wd_ref, o_ref)


def _post_odd_kernel(x_ref, mix_ref, wo_ref, g_ref, gate1_ref, sh2_ref, sc2_ref, gate2_ref,
                     wg_ref, wu_ref, wd_ref, o_ref):
    y = jnp.dot(mix_ref[...], wo_ref[...], preferred_element_type=F32)
    _ffn_tail(x_ref[...], y, g_ref, gate1_ref, sh2_ref, sc2_ref, gate2_ref, wg_ref, wu_ref, wd_ref, o_ref)


def _tail_specs(d, d_ff):
    vec = pl.BlockSpec((None, 1, d), lambda i, t: (i, 0, 0))
    return [_const_spec((d, d)), _const_spec((3, d)), vec, vec, vec, vec,
            _const_spec((d, d_ff)), _const_spec((d, d_ff)), _const_spec((d_ff, d))]


def _post_even(x, z, branches, expand, pool_w, pool_scale, w_out, g3, gate1, sh2, sc2, gate2, wg, wu, wd, tm):
    b, l, d = x.shape
    d_ff = wg.shape[1]
    (o1, l1), (o2, l2), (o3, l3) = branches
    halo_blocks = l // B_HALO
    per_tile = tm // B_HALO
    tok = lambda w: pl.BlockSpec((None, tm, w), lambda i, t: (i, t, 0))
    u_col = 1
    in_specs = [tok(d), tok(A_WIDTH), tok(A_WIDTH), tok(A_WIDTH), tok(LSE_LANES), tok(LSE_LANES), tok(LSE_LANES),
                pl.BlockSpec((None, B_HALO, B_WIDTH), lambda i, t: (i, jnp.maximum(t * per_tile - 1, 0), u_col)),
                pl.BlockSpec((None, tm, B_WIDTH), lambda i, t: (i, t, u_col)),
                pl.BlockSpec((None, B_HALO, B_WIDTH),
                             lambda i, t: (i, jnp.minimum((t + 1) * per_tile, halo_blocks - 1), u_col)),
                _const_spec((LSE_LANES, A_WIDTH)), _const_spec((B_GROUPS, B_GROUP_DIM, B_GROUP_DIM)),
                _const_spec((1, B_WIDTH))] + _tail_specs(d, d_ff)
    return pl.pallas_call(
        functools.partial(_post_even_kernel, tm=tm, seq_len=l),
        grid=(b, l // tm),
        in_specs=in_specs,
        out_specs=tok(d),
        out_shape=jax.ShapeDtypeStruct((b, l, d), F32),
        scratch_shapes=[pltpu.VMEM((tm + 2 * B_HALO, B_WIDTH), F32)],
        compiler_params=_params(2),
        name="post_even",
    )(x, o1, o2, o3, l1, l2, l3, z, z, z, expand, pool_w, pool_scale.reshape(1, B_WIDTH),
      w_out, g3, gate1, sh2, sc2, gate2, wg, wu, wd)


def _post_odd(x, mix, w_out, g3, gate1, sh2, sc2, gate2, wg, wu, wd, tm):
    b, l, d = x.shape
    d_ff = wg.shape[1]
    tok = lambda w: pl.BlockSpec((None, tm, w), lambda i, t: (i, t, 0))
    return pl.pallas_call(
        _post_odd_kernel,
        grid=(b, l // tm),
        in_specs=[tok(d), tok(C_WIDTH)] + _tail_specs(d, d_ff),
        out_specs=tok(d),
        out_shape=jax.ShapeDtypeStruct((b, l, d), F32),
        compiler_params=_params(2),
        name="post_odd",
    )(x, mix, w_out, g3, gate1, sh2, sc2, gate2, wg, wu, wd)


def _head_expand_matrix():
    rows = np.arange(LSE_LANES)[:, None]
    cols = np.arange(A_WIDTH)[None, :]
    return jnp.asarray(rows == (cols // HEAD_DIM) * (LSE_LANES // A_HEADS), BF16)


def kernel(x_prompt, x_sample, c_prompt, c_sample, norm_g, ada_w, ada_b, ffn_w_gate, ffn_w_up, ffn_w_down,
           even_w_in, even_w_out, pool_w, pool_scale, t5_table, odd_w_qkv, odd_w_out, odd_rpb):
    depth = norm_g.shape[0]
    d = x_prompt.shape[-1]
    nb_prompt = c_prompt.shape[0]
    mod_all = _ada(jnp.concatenate([c_prompt, c_sample], axis=0), ada_w, ada_b)
    expand = _head_expand_matrix()
    biases = [_dilated_bias(t5_table, dil) for _, dil in A_BRANCHES]

    def trunk(x, mod, tm):
        b = x.shape[0]
        for layer in range(depth):
            sh1, sc1, g1, sh2, sc2, g2 = [m.reshape(b, 1, d) for m in jnp.split(mod[layer], 6, axis=-1)]
            g = norm_g[layer]
            i = layer // 2
            ffn = (ffn_w_gate[layer].astype(BF16), ffn_w_up[layer].astype(BF16), ffn_w_down[layer].astype(BF16))
            if layer % 2 == 0:
                wq, wk, wv, wu = jnp.split(even_w_in[i], [A_WIDTH, 2 * A_WIDTH, 3 * A_WIDTH], axis=-1)
                w_in = jnp.concatenate([wq * ATTN_SCALE, wu, wk, wv], axis=-1).astype(BF16)
                z = _pre(x, g[0], sh1, sc1, w_in, tm)
                branches = [_dilated_branch(z, bias, dil) for bias, (_, dil) in zip(biases, A_BRANCHES)]
                x = _post_even(x, z, branches, expand, pool_w[i].astype(BF16), pool_scale[i],
                               even_w_out[i].astype(BF16), g[1:4], g1, sh2, sc2, g2, *ffn, tm)
            else:
                wq, wk, wv = jnp.split(odd_w_qkv[i], 3, axis=-1)
                w_in = jnp.concatenate([wq * ATTN_SCALE, wk, wv], axis=-1).astype(BF16)
                z = _pre(x, g[0], sh1, sc1, w_in, tm)
                mix = _natten(z, _natten_bias(odd_rpb[i]))
                x = _post_odd(x, mix, odd_w_out[i].astype(BF16), g[1:4], g1, sh2, sc2, g2, *ffn, tm)
        return x

    y_prompt = trunk(x_prompt, mod_all[:, :nb_prompt], 256)
    y_sample = trunk(x_sample, mod_all[:, nb_prompt:], 256)
    return (y_prompt, y_sample)
```

```python
import functools

import numpy as np
import jax
import jax.numpy as jnp
from jax import lax
from jax.experimental import pallas as pl
from jax.experimental.pallas import tpu as pltpu

D_MODEL = 1024
HEAD_DIM = 64
ATTN_SCALE = HEAD_DIM ** -0.5
A_HEADS = 8
A_WIDTH = A_HEADS * HEAD_DIM
A_DILATIONS = (1, 4, 16)
A_BLOCK = 128
A_RADIUS = 64
A_CHUNK = 512
B_GROUPS = 4
B_WINDOWS = (2, 4, 8, 16)
B_WIDTH = 512
B_GROUP_DIM = 128
B_HALO = 16
EVEN_IN = 3 * A_WIDTH + B_WIDTH
C_HEADS = 16
C_WIDTH = C_HEADS * HEAD_DIM
GRID_W = 64
NA_ROWS = 8
NA_COLS = 16
NA_CHUNK_ROWS = 8
NA_QUAD = 4
NA_ROWOFFS = 2 * NA_ROWS - 1
T5_BUCKETS = 32
T5_MAX_DIST = 1024
NEG = -1e30
EPS = 1e-6
LANES = 128
LSE_LANES = 128
TOKEN_TILE = 256
V7X_VMEM_LIMIT_BYTES = 56 * 1024 * 1024

BF16 = jnp.bfloat16
F32 = jnp.float32


def _const_spec(shape):
    zeros = (0,) * len(shape)
    return pl.BlockSpec(shape, lambda *_: zeros, pipeline_mode=pl.Buffered(1))


def _params(n_axes):
    return pltpu.CompilerParams(dimension_semantics=("arbitrary",) * n_axes,
                                vmem_limit_bytes=V7X_VMEM_LIMIT_BYTES)


def _rms(x, g):
    return x * lax.rsqrt(jnp.mean(x * x, axis=-1, keepdims=True) + EPS) * g


def _ada_kernel(c_ref, w_ref, b_ref, o_ref):
    c = c_ref[...]
    s = (c * jax.nn.sigmoid(c)).astype(BF16)
    o_ref[...] = jnp.dot(s, w_ref[...].astype(BF16), preferred_element_type=F32) + b_ref[...]


def _ada(c_all, ada_w, ada_b):
    depth, d, six_d = ada_w.shape
    nb = c_all.shape[0]
    return pl.pallas_call(
        _ada_kernel,
        grid=(depth, six_d // d),
        in_specs=[pl.BlockSpec((nb, d), lambda l, j: (0, 0)),
                  pl.BlockSpec((None, d, d), lambda l, j: (l, 0, j)),
                  pl.BlockSpec((None, 1, d), lambda l, j: (l, 0, j))],
        out_specs=pl.BlockSpec((None, nb, d), lambda l, j: (l, 0, j)),
        out_shape=jax.ShapeDtypeStruct((depth, nb, six_d), F32),
        compiler_params=_params(2),
        name="ada_mod",
    )(c_all, ada_w, ada_b.reshape(depth, 1, six_d))


def _modulated(x_ref, g_ref, sh_ref, sc_ref):
    return (_rms(x_ref[...], g_ref[...]) * (1.0 + sc_ref[...]) + sh_ref[...]).astype(BF16)


def _pre_odd_kernel(x_ref, g_ref, sh_ref, sc_ref, w_ref, o_ref, *, col_chunk):
    hb = _modulated(x_ref, g_ref, sh_ref, sc_ref)
    for j in range(w_ref.shape[1] // col_chunk):
        cols = slice(j * col_chunk, (j + 1) * col_chunk)
        o_ref[:, cols] = jnp.dot(hb, w_ref[:, cols], preferred_element_type=F32).astype(BF16)


def _pre_even_kernel(x_ref, g_ref, sh_ref, sc_ref, w_ref, z1_ref, z4_ref, z16_ref, zs, *, col_chunk, tm):
    hb = _modulated(x_ref, g_ref, sh_ref, sc_ref)
    per_chunk = col_chunk // LANES
    for j in range(w_ref.shape[1] // col_chunk):
        cols = slice(j * col_chunk, (j + 1) * col_chunk)
        z = jnp.dot(hb, w_ref[:, cols], preferred_element_type=F32)
        z1_ref[:, cols] = z.astype(BF16)
        for cb in range(per_chunk):
            zs[j * per_chunk + cb] = z[:, cb * LANES:(cb + 1) * LANES]
    for dil, ref in ((4, z4_ref), (16, z16_ref)):
        for rho in range(dil):
            ref[rho] = jnp.concatenate([zs[cb, pl.ds(rho, tm // dil, stride=dil), :]
                                        for cb in range(3 * A_WIDTH // LANES)], axis=-1).astype(BF16)


def _pre_specs(d, n, tm):
    vec = pl.BlockSpec((None, 1, d), lambda i, t: (i, 0, 0))
    return [pl.BlockSpec((None, tm, d), lambda i, t: (i, t, 0)), _const_spec((1, d)), vec, vec,
            _const_spec((d, n))]


def _pre_odd(x, g, shift, scale, w, tm):
    b, l, d = x.shape
    n = w.shape[1]
    return pl.pallas_call(
        functools.partial(_pre_odd_kernel, col_chunk=1024),
        grid=(b, l // tm),
        in_specs=_pre_specs(d, n, tm),
        out_specs=pl.BlockSpec((None, tm, n), lambda i, t: (i, t, 0)),
        out_shape=jax.ShapeDtypeStruct((b, l, n), BF16),
        compiler_params=_params(2),
        name="pre_odd",
    )(x, g.reshape(1, d), shift, scale, w)


def _pre_even(x, g, shift, scale, w, tm):
    b, l, d = x.shape
    n = w.shape[1]
    qkv = 3 * A_WIDTH
    return pl.pallas_call(
        functools.partial(_pre_even_kernel, col_chunk=1024, tm=tm),
        grid=(b, l // tm),
        in_specs=_pre_specs(d, n, tm),
        out_specs=[pl.BlockSpec((None, tm, n), lambda i, t: (i, t, 0)),
                   pl.BlockSpec((None, 4, tm // 4, qkv), lambda i, t: (i, 0, t, 0)),
                   pl.BlockSpec((None, 16, tm // 16, qkv), lambda i, t: (i, 0, t, 0))],
        out_shape=[jax.ShapeDtypeStruct((b, l, n), BF16),
                   jax.ShapeDtypeStruct((b, 4, l // 4, qkv), BF16),
                   jax.ShapeDtypeStruct((b, 16, l // 16, qkv), BF16)],
        scratch_shapes=[pltpu.VMEM((n // LANES, tm, LANES), F32)],
        compiler_params=_params(2),
        name="pre_even",
    )(x, g.reshape(1, d), shift, scale, w)


def _t5_bucket(rel):
    nb = T5_BUCKETS // 2
    max_exact = nb // 2
    ret = (rel > 0).astype(np.int32) * nb
    n = np.abs(rel)
    large = max_exact + (np.log(np.maximum(n, 1) / max_exact) / np.log(T5_MAX_DIST / max_exact)
                         * (nb - max_exact)).astype(np.int32)
    large = np.minimum(large, nb - 1)
    return (ret + np.where(n < max_exact, n, large)).astype(np.int32)


def _dilated_bias(t5_table, dilation):
    kw = A_BLOCK + 2 * A_RADIUS
    rel = np.arange(kw)[:, None] - A_RADIUS - np.arange(A_BLOCK)[None, :]
    bias = jnp.transpose(t5_table[_t5_bucket(rel * dilation)], (2, 0, 1)).astype(F32)
    band = np.abs(rel) <= A_RADIUS
    variants = []
    for v in range(4):
        row_ok = np.ones((kw, 1), bool)
        if v & 1:
            row_ok[:A_RADIUS] = False
        if v & 2:
            row_ok[kw - A_RADIUS:] = False
        masked = jnp.where((band & row_ok)[None], bias, NEG)
        variants.append(masked.reshape(A_HEADS // 2, 2, kw, A_BLOCK).transpose(0, 2, 1, 3)
                        .reshape(A_HEADS // 2, kw, 2 * A_BLOCK))
    return jnp.stack(variants, axis=0)


def _dilated_kernel(q_ref, kvp_ref, kvc_ref, kvn_ref, bias_ref, o_ref, lse_ref, kv_buf, *, chunk, sub_len, group):
    kw = A_BLOCK + 2 * A_RADIUS
    pair_w = 2 * HEAD_DIM
    n_pairs = A_HEADS // 2
    kv_buf[:, 0:A_RADIUS] = kvp_ref[...]
    kv_buf[:, A_RADIUS:A_RADIUS + chunk] = kvc_ref[...]
    kv_buf[:, A_RADIUS + chunk:] = kvn_ref[...]
    chunk_start = pl.program_id(2) * chunk
    first_head = lax.broadcasted_iota(jnp.int32, (A_BLOCK, pair_w), 1) < HEAD_DIM
    items = [(rl, j, jp) for rl in range(group) for j in range(chunk // A_BLOCK) for jp in range(n_pairs)]

    def scores(rl, j, jp):
        row0 = j * A_BLOCK
        pos = chunk_start + row0
        variant = (pos == 0).astype(jnp.int32) + 2 * (pos + A_BLOCK == sub_len).astype(jnp.int32)
        cols = slice(jp * pair_w, (jp + 1) * pair_w)
        qp = q_ref[rl, row0:row0 + A_BLOCK, cols].astype(F32)
        qbd = jnp.concatenate([jnp.where(first_head, qp, 0.0), jnp.where(first_head, 0.0, qp)],
                              axis=0).astype(BF16)
        kp = kv_buf[rl, row0:row0 + kw, cols]
        s = lax.dot_general(kp, qbd, (((1,), (1,)), ((), ())), preferred_element_type=F32)
        return s + bias_ref[variant, jp]

    s_next = scores(*items[0])
    lse_rows = []
    for idx, (rl, j, jp) in enumerate(items):
        row0 = j * A_BLOCK
        cols = slice(jp * pair_w, (jp + 1) * pair_w)
        vcols = slice(A_WIDTH + jp * pair_w, A_WIDTH + (jp + 1) * pair_w)
        vp = kv_buf[rl, row0:row0 + kw, vcols]
        s = s_next
        if idx + 1 < len(items):
            s_next = scores(*items[idx + 1])
        m = jnp.max(s, axis=0, keepdims=True)
        p = jnp.exp(s - m)
        den = jnp.sum(p, axis=0, keepdims=True)
        res = lax.dot_general(vp, p.astype(BF16), (((0,), (0,)), ((), ())), preferred_element_type=F32)
        res = res * (1.0 / den)
        o_t = jnp.concatenate([res[0:HEAD_DIM, 0:A_BLOCK], res[HEAD_DIM:pair_w, A_BLOCK:2 * A_BLOCK]], axis=0)
        o_ref[rl, row0:row0 + A_BLOCK, cols] = o_t.T.astype(BF16)
        lse = m + jnp.log(den)
        lse_rows += [lse[:, 0:A_BLOCK], lse[:, A_BLOCK:2 * A_BLOCK]]
        if jp == n_pairs - 1:
            lse_t = jnp.concatenate(lse_rows + [jnp.zeros((LSE_LANES - A_HEADS, A_BLOCK), F32)], axis=0)
            lse_ref[rl, row0:row0 + A_BLOCK, :] = lse_t.T
            lse_rows = []


def _dilated_branch(zd, bias):
    b, dilation, sub_len, _ = zd.shape
    chunk = min(A_CHUNK, sub_len)
    group = min(dilation, A_CHUNK // chunk)
    halo_blocks = sub_len // A_RADIUS
    per_chunk = chunk // A_RADIUS
    kw = A_BLOCK + 2 * A_RADIUS
    kv_w = 2 * A_WIDTH
    return pl.pallas_call(
        functools.partial(_dilated_kernel, chunk=chunk, sub_len=sub_len, group=group),
        grid=(b, dilation // group, sub_len // chunk),
        in_specs=[
            pl.BlockSpec((None, group, chunk, A_WIDTH), lambda i, r, c: (i, r, c, 2)),
            pl.BlockSpec((None, group, A_RADIUS, kv_w),
                         lambda i, r, c: (i, r, jnp.maximum(c * per_chunk - 1, 0), 0)),
            pl.BlockSpec((None, group, chunk, kv_w), lambda i, r, c: (i, r, c, 0)),
            pl.BlockSpec((None, group, A_RADIUS, kv_w),
                         lambda i, r, c: (i, r, jnp.minimum((c + 1) * per_chunk, halo_blocks - 1), 0)),
            _const_spec((4, A_HEADS // 2, kw, 2 * A_BLOCK)),
        ],
        out_specs=[pl.BlockSpec((None, group, chunk, A_WIDTH), lambda i, r, c: (i, r, c, 0)),
                   pl.BlockSpec((None, group, chunk, LSE_LANES), lambda i, r, c: (i, r, c, 0))],
        out_shape=[jax.ShapeDtypeStruct((b, dilation, sub_len, A_WIDTH), BF16),
                   jax.ShapeDtypeStruct((b, dilation, sub_len, LSE_LANES), F32)],
        scratch_shapes=[pltpu.VMEM((group, chunk + 2 * A_RADIUS, kv_w), BF16)],
        compiler_params=_params(3),
        name=f"dilated_d{dilation}",
    )(zd, zd, zd, zd, bias)


def _natten_bias(rpb):
    c = np.arange(GRID_W)
    cstart = np.clip(c - NA_COLS // 2, 0, GRID_W - NA_COLS)
    colmask = (c[None, :] >= cstart[:, None]) & (c[None, :] < cstart[:, None] + NA_COLS)
    coloff = np.clip(c[None, :] - c[:, None] + NA_COLS - 1, 0, 2 * NA_COLS - 2)
    bias = jnp.where(colmask[None, None], rpb[:, :, coloff].astype(F32), NEG)
    bias = jnp.transpose(bias, (0, 1, 3, 2))
    bias = bias.reshape(C_HEADS // NA_QUAD, NA_QUAD, NA_ROWOFFS, GRID_W, GRID_W)
    return jnp.transpose(bias, (0, 2, 3, 1, 4)).reshape(C_HEADS // NA_QUAD, NA_ROWOFFS, GRID_W, NA_QUAD * GRID_W)


def _natten_kernel(q_ref, kp_ref, kc_ref, kn_ref, vp_ref, vc_ref, vn_ref, bias_ref, o_ref, k_buf, v_buf,
                   *, grid_rows):
    halo = (NA_ROWS // 2) * GRID_W
    chunk = NA_CHUNK_ROWS * GRID_W
    nkeys = NA_ROWS * GRID_W
    quad_w = NA_QUAD * HEAD_DIM
    pair_w = 2 * HEAD_DIM
    k_buf[0:halo] = kp_ref[...]
    k_buf[halo:halo + chunk] = kc_ref[...]
    k_buf[halo + chunk:] = kn_ref[...]
    v_buf[0:halo] = vp_ref[...]
    v_buf[halo:halo + chunk] = vc_ref[...]
    v_buf[halo + chunk:] = vn_ref[...]
    row_base = pl.program_id(1) * NA_CHUNK_ROWS
    lane_head = lax.broadcasted_iota(jnp.int32, (GRID_W, quad_w), 1) // HEAD_DIM
    first_head = lax.broadcasted_iota(jnp.int32, (GRID_W, pair_w), 1) < HEAD_DIM

    def window(rr):
        r = row_base + rr
        rs = jnp.clip(r - NA_ROWS // 2, 0, grid_rows - NA_ROWS)
        return rs - r + (NA_ROWS - 1), pl.multiple_of((rs - (row_base - NA_ROWS // 2)) * GRID_W, GRID_W)

    windows = [window(rr) for rr in range(NA_CHUNK_ROWS)]
    items = [(rr, g) for rr in range(NA_CHUNK_ROWS) for g in range(C_HEADS // NA_QUAD)]

    def scores(rr, g):
        ro0, k0 = windows[rr]
        cols = slice(g * quad_w, (g + 1) * quad_w)
        qs = q_ref[rr * GRID_W:(rr + 1) * GRID_W, cols].astype(F32)
        qbd = jnp.concatenate([jnp.where(lane_head == hh, qs, 0.0) for hh in range(NA_QUAD)],
                              axis=0).astype(BF16)
        kq = k_buf[pl.ds(k0, nkeys), cols]
        s = lax.dot_general(kq, qbd, (((1,), (1,)), ((), ())), preferred_element_type=F32)
        return jnp.concatenate([s[i * GRID_W:(i + 1) * GRID_W] + bias_ref[g, ro0 + i]
                                for i in range(NA_ROWS)], axis=0)

    s_next = scores(*items[0])
    for idx, (rr, g) in enumerate(items):
        cols = slice(g * quad_w, (g + 1) * quad_w)
        vq = v_buf[pl.ds(windows[rr][1], nkeys), cols]
        s = s_next
        if idx + 1 < len(items):
            s_next = scores(*items[idx + 1])
        m = jnp.max(s, axis=0, keepdims=True)
        p = jnp.exp(s - m)
        den = jnp.sum(p, axis=0, keepdims=True)
        res = lax.dot_general(vq, p.astype(BF16), (((0,), (0,)), ((), ())), preferred_element_type=F32)
        res = res * (1.0 / den)
        slabs = []
        for half in range(NA_QUAD // 2):
            blk = res[half * pair_w:(half + 1) * pair_w, half * pair_w:(half + 1) * pair_w].T
            slabs.append(jnp.where(first_head, blk[0:GRID_W], blk[GRID_W:2 * GRID_W]))
        o_ref[rr * GRID_W:(rr + 1) * GRID_W, cols] = jnp.concatenate(slabs, axis=-1).astype(BF16)


def _natten(z, bias):
    b, l, _ = z.shape
    grid_rows = l // GRID_W
    chunk = NA_CHUNK_ROWS * GRID_W
    halo = (NA_ROWS // 2) * GRID_W
    halo_blocks = l // halo
    per_chunk = chunk // halo

    def cur(col):
        return pl.BlockSpec((None, chunk, C_WIDTH), lambda i, c: (i, c, col))

    def prev(col):
        return pl.BlockSpec((None, halo, C_WIDTH), lambda i, c: (i, jnp.maximum(c * per_chunk - 1, 0), col))

    def nxt(col):
        return pl.BlockSpec((None, halo, C_WIDTH),
                            lambda i, c: (i, jnp.minimum((c + 1) * per_chunk, halo_blocks - 1), col))

    return pl.pallas_call(
        functools.partial(_natten_kernel, grid_rows=grid_rows),
        grid=(b, l // chunk),
        in_specs=[cur(0), prev(1), cur(1), nxt(1), prev(2), cur(2), nxt(2),
                  _const_spec((C_HEADS // NA_QUAD, NA_ROWOFFS, GRID_W, NA_QUAD * GRID_W))],
        out_specs=pl.BlockSpec((None, chunk, C_WIDTH), lambda i, c: (i, c, 0)),
        out_shape=jax.ShapeDtypeStruct((b, l, C_WIDTH), BF16),
        scratch_shapes=[pltpu.VMEM((chunk + 2 * halo, C_WIDTH), BF16),
                        pltpu.VMEM((chunk + 2 * halo, C_WIDTH), BF16)],
        compiler_params=_params(2),
        name="natten",
    )(z, z, z, z, z, z, z, bias)


def _ffn_tail(x, y, g_ref, gate1_ref, sh2_ref, sc2_ref, gate2_ref, wg_ref, wu_ref, wd_ref, o_ref):
    x1 = x + gate1_ref[...] * _rms(y, g_ref[0:1])
    h = (_rms(x1, g_ref[1:2]) * (1.0 + sc2_ref[...]) + sh2_ref[...]).astype(BF16)
    gate = jnp.dot(h, wg_ref[...], preferred_element_type=F32)
    up = jnp.dot(h, wu_ref[...], preferred_element_type=F32)
    act = (gate * jax.nn.sigmoid(gate) * up).astype(BF16)
    y2 = jnp.dot(act, wd_ref[...], preferred_element_type=F32)
    o_ref[...] = x1 + gate2_ref[...] * _rms(y2, g_ref[2:3])


def _token_order(src_ref, buf, dilation, tm):
    n_slabs = buf.shape[0]
    for rho in range(dilation):
        rows = src_ref[rho].astype(F32)
        for cb in range(n_slabs):
            buf[cb, pl.ds(rho, tm // dilation, stride=dilation), :] = rows[:, cb * LANES:(cb + 1) * LANES]
    return jnp.concatenate([buf[cb] for cb in range(n_slabs)], axis=-1)


def _post_even_kernel(x_ref, o1_ref, o4_ref, o16_ref, l1_ref, l4_ref, l16_ref, up_ref, uc_ref, un_ref,
                      expand_ref, pw_ref, ps_ref, wo_ref, g_ref, gate1_ref, sh2_ref, sc2_ref, gate2_ref,
                      wg_ref, wu_ref, wd_ref, o_ref, u_buf, o4_buf, o16_buf, l4_buf, l16_buf, *, tm, seq_len):
    outs = (o1_ref[...].astype(F32), _token_order(o4_ref, o4_buf, 4, tm), _token_order(o16_ref, o16_buf, 16, tm))
    l1, l2, l3 = l1_ref[...], _token_order(l4_ref, l4_buf, 4, tm), _token_order(l16_ref, l16_buf, 16, tm)
    mx = jnp.maximum(jnp.maximum(l1, l2), l3)
    es = (jnp.exp(l1 - mx), jnp.exp(l2 - mx), jnp.exp(l3 - mx))
    inv = 1.0 / (es[0] + es[1] + es[2])
    ya = jnp.zeros((tm, A_WIDTH), F32)
    for e, o_br in zip(es, outs):
        wts = jnp.dot((e * inv).astype(BF16), expand_ref[...], preferred_element_type=F32)
        ya = ya + wts * o_br

    t0 = pl.program_id(1) * tm
    u_buf[0:B_HALO] = jnp.where(t0 > 0, up_ref[...].astype(F32), 0.0)
    u_buf[B_HALO:B_HALO + tm] = uc_ref[...].astype(F32)
    u_buf[B_HALO + tm:] = jnp.where(t0 + tm < seq_len, un_ref[...].astype(F32), 0.0)
    t = t0 + lax.broadcasted_iota(jnp.int32, (tm, 1), 0)
    yb = []
    for g, w in enumerate(B_WINDOWS):
        cols = slice(g * B_GROUP_DIM, (g + 1) * B_GROUP_DIM)
        total = jnp.zeros((tm, B_GROUP_DIM), F32)
        for j in range(-(w // 2), w // 2):
            total = total + u_buf[B_HALO + j:B_HALO + j + tm, cols]
        cnt = (jnp.clip(t + w // 2, 0, seq_len) - jnp.clip(t - w // 2, 0, seq_len)).astype(F32)
        pooled = total / cnt - u_buf[B_HALO:B_HALO + tm, cols]
        yg = jnp.dot(pooled.astype(BF16), pw_ref[g], preferred_element_type=F32)
        yb.append(yg * ps_ref[:, cols])
    mix = jnp.concatenate([ya] + yb, axis=-1).astype(BF16)
    y = jnp.dot(mix, wo_ref[...], preferred_element_type=F32)
    _ffn_tail(x_ref[...], y, g_ref, gate1_ref, sh2_ref, sc2_ref, gate2_ref, wg_ref, wu_ref, wd_ref, o_ref)


def _post_odd_kernel(x_ref, mix_ref, wo_ref, g_ref, gate1_ref, sh2_ref, sc2_ref, gate2_ref,
                     wg_ref, wu_ref, wd_ref, o_ref):
    y = jnp.dot(mix_ref[...], wo_ref[...], preferred_element_type=F32)
    _ffn_tail(x_ref[...], y, g_ref, gate1_ref, sh2_ref, sc2_ref, gate2_ref, wg_ref, wu_ref, wd_ref, o_ref)


def _tail_specs(d, d_ff):
    vec = pl.BlockSpec((None, 1, d), lambda i, t: (i, 0, 0))
    return [_const_spec((d, d)), _const_spec((3, d)), vec, vec, vec, vec,
            _const_spec((d, d_ff)), _const_spec((d, d_ff)), _const_spec((d_ff, d))]


def _post_even(x, z1, branches, expand, pool_w, pool_scale, w_out, g3, gate1, sh2, sc2, gate2, wg, wu, wd, tm):
    b, l, d = x.shape
    d_ff = wg.shape[1]
    (o1, l1), (o4, l4), (o16, l16) = branches
    halo_blocks = l // B_HALO
    per_tile = tm // B_HALO
    tok = lambda w: pl.BlockSpec((None, tm, w), lambda i, t: (i, t, 0))
    grouped = lambda dil, w: pl.BlockSpec((None, dil, tm // dil, w), lambda i, t: (i, 0, t, 0))
    u_col = 3
    in_specs = [tok(d), tok(A_WIDTH), grouped(4, A_WIDTH), grouped(16, A_WIDTH),
                tok(LSE_LANES), grouped(4, LSE_LANES), grouped(16, LSE_LANES),
                pl.BlockSpec((None, B_HALO, B_WIDTH), lambda i, t: (i, jnp.maximum(t * per_tile - 1, 0), u_col)),
                pl.BlockSpec((None, tm, B_WIDTH), lambda i, t: (i, t, u_col)),
                pl.BlockSpec((None, B_HALO, B_WIDTH),
                             lambda i, t: (i, jnp.minimum((t + 1) * per_tile, halo_blocks - 1), u_col)),
                _const_spec((LSE_LANES, A_WIDTH)), _const_spec((B_GROUPS, B_GROUP_DIM, B_GROUP_DIM)),
                _const_spec((1, B_WIDTH))] + _tail_specs(d, d_ff)
    return pl.pallas_call(
        functools.partial(_post_even_kernel, tm=tm, seq_len=l),
        grid=(b, l // tm),
        in_specs=in_specs,
        out_specs=tok(d),
        out_shape=jax.ShapeDtypeStruct((b, l, d), F32),
        scratch_shapes=[pltpu.VMEM((tm + 2 * B_HALO, B_WIDTH), F32),
                        pltpu.VMEM((A_WIDTH // LANES, tm, LANES), F32),
                        pltpu.VMEM((A_WIDTH // LANES, tm, LANES), F32),
                        pltpu.VMEM((LSE_LANES // LANES, tm, LANES), F32),
                        pltpu.VMEM((LSE_LANES // LANES, tm, LANES), F32)],
        compiler_params=_params(2),
        name="post_even",
    )(x, o1.reshape(b, l, A_WIDTH), o4, o16, l1.reshape(b, l, LSE_LANES), l4, l16, z1, z1, z1,
      expand, pool_w, pool_scale.reshape(1, B_WIDTH), w_out, g3, gate1, sh2, sc2, gate2, wg, wu, wd)


def _post_odd(x, mix, w_out, g3, gate1, sh2, sc2, gate2, wg, wu, wd, tm):
    b, l, d = x.shape
    d_ff = wg.shape[1]
    tok = lambda w: pl.BlockSpec((None, tm, w), lambda i, t: (i, t, 0))
    return pl.pallas_call(
        _post_odd_kernel,
        grid=(b, l // tm),
        in_specs=[tok(d), tok(C_WIDTH)] + _tail_specs(d, d_ff),
        out_specs=tok(d),
        out_shape=jax.ShapeDtypeStruct((b, l, d), F32),
        compiler_params=_params(2),
        name="post_odd",
    )(x, mix, w_out, g3, gate1, sh2, sc2, gate2, wg, wu, wd)


def _head_expand_matrix():
    rows = np.arange(LSE_LANES)[:, None]
    cols = np.arange(A_WIDTH)[None, :]
    return jnp.asarray(rows == cols // HEAD_DIM, BF16)


def kernel(x_prompt, x_sample, c_prompt, c_sample, norm_g, ada_w, ada_b, ffn_w_gate, ffn_w_up, ffn_w_down,
           even_w_in, even_w_out, pool_w, pool_scale, t5_table, odd_w_qkv, odd_w_out, odd_rpb):
    depth = norm_g.shape[0]
    d = x_prompt.shape[-1]
    nb_prompt = c_prompt.shape[0]
    mod_all = _ada(jnp.concatenate([c_prompt, c_sample], axis=0), ada_w, ada_b)
    expand = _head_expand_matrix()
    biases = [_dilated_bias(t5_table, dil) for dil in A_DILATIONS]

    layers = []
    for layer in range(depth):
        i = layer // 2
        w = dict(ffn=(ffn_w_gate[layer].astype(BF16), ffn_w_up[layer].astype(BF16), ffn_w_down[layer].astype(BF16)))
        if layer % 2 == 0:
            wq, wk, wv, wu = jnp.split(even_w_in[i], [A_WIDTH, 2 * A_WIDTH, 3 * A_WIDTH], axis=-1)
            w.update(w_in=jnp.concatenate([wk, wv, wq * ATTN_SCALE, wu], axis=-1).astype(BF16),
                     w_out=even_w_out[i].astype(BF16), pool_w=pool_w[i].astype(BF16), pool_scale=pool_scale[i])
        else:
            wq, wk, wv = jnp.split(odd_w_qkv[i], 3, axis=-1)
            w.update(w_in=jnp.concatenate([wq * ATTN_SCALE, wk, wv], axis=-1).astype(BF16),
                     w_out=odd_w_out[i].astype(BF16), rpb=_natten_bias(odd_rpb[i]))
        layers.append(w)

    def trunk(x, mod, tm):
        b, l, _ = x.shape
        for layer, w in enumerate(layers):
            sh1, sc1, g1, sh2, sc2, g2 = [m.reshape(b, 1, d) for m in jnp.split(mod[layer], 6, axis=-1)]
            g = norm_g[layer]
            if layer % 2 == 0:
                z1, z4, z16 = _pre_even(x, g[0], sh1, sc1, w["w_in"], tm)
                branches = [_dilated_branch(zd, bias)
                            for zd, bias in zip((z1.reshape(b, 1, l, EVEN_IN), z4, z16), biases)]
                x = _post_even(x, z1, branches, expand, w["pool_w"], w["pool_scale"], w["w_out"],
                               g[1:4], g1, sh2, sc2, g2, *w["ffn"], tm)
            else:
                z = _pre_odd(x, g[0], sh1, sc1, w["w_in"], tm)
                mix = _natten(z, w["rpb"])
                x = _post_odd(x, mix, w["w_out"], g[1:4], g1, sh2, sc2, g2, *w["ffn"], tm)
        return x

    y_prompt = trunk(x_prompt, mod_all[:, :nb_prompt], TOKEN_TILE)
    y_sample = trunk(x_sample, mod_all[:, nb_prompt:], TOKEN_TILE)
    return (y_prompt, y_sample)
```

```python
import functools

import numpy as np
import jax
import jax.numpy as jnp
from jax import lax
from jax.experimental import pallas as pl
from jax.experimental.pallas import tpu as pltpu

D_MODEL = 1024
HEAD_DIM = 64
ATTN_SCALE = HEAD_DIM ** -0.5
LOG2E = float(np.log2(np.e))
LN2 = float(np.log(2.0))
Q_SCALE = ATTN_SCALE * LOG2E
A_HEADS = 8
A_WIDTH = A_HEADS * HEAD_DIM
A_DILATIONS = (1, 4, 16)
A_BLOCK = 128
A_RADIUS = 64
A_CHUNK = 512
A_LOOKAHEAD = 4
B_GROUPS = 4
B_WINDOWS = (2, 4, 8, 16)
B_WIDTH = 512
B_GROUP_DIM = 128
B_HALO = 16
EVEN_IN = 3 * A_WIDTH + B_WIDTH
C_HEADS = 16
C_WIDTH = C_HEADS * HEAD_DIM
GRID_W = 64
NA_ROWS = 8
NA_COLS = 16
NA_CHUNK_ROWS = 8
NA_QUAD = 4
NA_ROWOFFS = 2 * NA_ROWS - 1
NA_LOOKAHEAD = 2
T5_BUCKETS = 32
T5_MAX_DIST = 1024
NEG = -1e30
EPS = 1e-6
LANES = 128
LSE_LANES = 128
TOKEN_TILE = 512
TAIL_SUBTILES = 2
PRE_SUBTILES = 2
FFN_CHUNK = 1536
V7X_VMEM_LIMIT_BYTES = 60 * 1024 * 1024

BF16 = jnp.bfloat16
F32 = jnp.float32


def _const_spec(shape):
    zeros = (0,) * len(shape)
    return pl.BlockSpec(shape, lambda *_: zeros, pipeline_mode=pl.Buffered(1))


def _params(n_axes):
    return pltpu.CompilerParams(dimension_semantics=("arbitrary",) * n_axes,
                                vmem_limit_bytes=V7X_VMEM_LIMIT_BYTES)


def _rms(x, g):
    return x * lax.rsqrt(jnp.mean(x * x, axis=-1, keepdims=True) + EPS) * g


def _ada_kernel(c_ref, w_ref, b_ref, o_ref):
    c = c_ref[...]
    s = (c * jax.nn.sigmoid(c)).astype(BF16)
    o_ref[...] = jnp.dot(s, w_ref[...].astype(BF16), preferred_element_type=F32) + b_ref[...]


def _ada(c_all, ada_w, ada_b):
    depth, d, six_d = ada_w.shape
    nb = c_all.shape[0]
    return pl.pallas_call(
        _ada_kernel,
        grid=(depth, six_d // d),
        in_specs=[pl.BlockSpec((nb, d), lambda l, j: (0, 0)),
                  pl.BlockSpec((None, d, d), lambda l, j: (l, 0, j)),
                  pl.BlockSpec((None, 1, d), lambda l, j: (l, 0, j))],
        out_specs=pl.BlockSpec((None, nb, d), lambda l, j: (l, 0, j)),
        out_shape=jax.ShapeDtypeStruct((depth, nb, six_d), F32),
        compiler_params=_params(2),
        name="ada_mod",
    )(c_all, ada_w, ada_b.reshape(depth, 1, six_d))


def _modulated(x_ref, g_ref, sh_ref, sc_ref):
    return (_rms(x_ref[...], g_ref[...]) * (1.0 + sc_ref[...]) + sh_ref[...]).astype(BF16)


def _pre_odd_kernel(x_ref, g_ref, sh_ref, sc_ref, w_ref, o_ref, *, col_chunk):
    hb = _modulated(x_ref, g_ref, sh_ref, sc_ref)
    for j in range(w_ref.shape[1] // col_chunk):
        cols = slice(j * col_chunk, (j + 1) * col_chunk)
        o_ref[:, cols] = jnp.dot(hb, w_ref[:, cols], preferred_element_type=F32).astype(BF16)


def _pre_even_kernel(x_ref, g_ref, sh_ref, sc_ref, w_ref, z1_ref, z4_ref, z16_ref, zs, *, col_chunk, tm):
    per_chunk = col_chunk // LANES
    rows = tm // PRE_SUBTILES
    items = [(sub, j) for sub in range(PRE_SUBTILES) for j in range(w_ref.shape[1] // col_chunk)]

    def regroup(sub, j):
        for cb in range(j * per_chunk, min((j + 1) * per_chunk, 3 * A_WIDTH // LANES)):
            for dil, ref in ((4, z4_ref), (16, z16_ref)):
                n = rows // dil
                for rho in range(dil):
                    ref[rho, sub * n:(sub + 1) * n, cb * LANES:(cb + 1) * LANES] = (
                        zs[cb, pl.ds(sub * rows + rho, n, stride=dil), :].astype(BF16))

    previous = None
    hbs = {}
    for sub, j in items:
        r = slice(sub * rows, (sub + 1) * rows)
        cols = slice(j * col_chunk, (j + 1) * col_chunk)
        if sub not in hbs:
            hbs[sub] = (_rms(x_ref[r, :], g_ref[...]) * (1.0 + sc_ref[...]) + sh_ref[...]).astype(BF16)
        z = jnp.dot(hbs[sub], w_ref[:, cols], preferred_element_type=F32)
        z1_ref[r, cols] = z.astype(BF16)
        for cb in range(per_chunk):
            zs[j * per_chunk + cb, r, :] = z[:, cb * LANES:(cb + 1) * LANES]
        if previous is not None:
            regroup(*previous)
        previous = (sub, j)
    regroup(*previous)


def _pre_specs(d, n, tm):
    vec = pl.BlockSpec((None, 1, d), lambda i, t: (i, 0, 0))
    return [pl.BlockSpec((None, tm, d), lambda i, t: (i, t, 0)), _const_spec((1, d)), vec, vec,
            _const_spec((d, n))]


def _pre_odd(x, g, shift, scale, w, tm):
    b, l, d = x.shape
    n = w.shape[1]
    return pl.pallas_call(
        functools.partial(_pre_odd_kernel, col_chunk=1024),
        grid=(b, l // tm),
        in_specs=_pre_specs(d, n, tm),
        out_specs=pl.BlockSpec((None, tm, n), lambda i, t: (i, t, 0)),
        out_shape=jax.ShapeDtypeStruct((b, l, n), BF16),
        compiler_params=_params(2),
        name="pre_odd",
    )(x, g.reshape(1, d), shift, scale, w)


def _pre_even(x, g, shift, scale, w, tm):
    b, l, d = x.shape
    n = w.shape[1]
    qkv = 3 * A_WIDTH
    return pl.pallas_call(
        functools.partial(_pre_even_kernel, col_chunk=1024, tm=tm),
        grid=(b, l // tm),
        in_specs=_pre_specs(d, n, tm),
        out_specs=[pl.BlockSpec((None, tm, n), lambda i, t: (i, t, 0)),
                   pl.BlockSpec((None, 4, tm // 4, qkv), lambda i, t: (i, 0, t, 0)),
                   pl.BlockSpec((None, 16, tm // 16, qkv), lambda i, t: (i, 0, t, 0))],
        out_shape=[jax.ShapeDtypeStruct((b, l, n), BF16),
                   jax.ShapeDtypeStruct((b, 4, l // 4, qkv), BF16),
                   jax.ShapeDtypeStruct((b, 16, l // 16, qkv), BF16)],
        scratch_shapes=[pltpu.VMEM((n // LANES, tm, LANES), F32)],
        compiler_params=_params(2),
        name="pre_even",
    )(x, g.reshape(1, d), shift, scale, w)


def _t5_bucket(rel):
    nb = T5_BUCKETS // 2
    max_exact = nb // 2
    ret = (rel > 0).astype(np.int32) * nb
    n = np.abs(rel)
    large = max_exact + (np.log(np.maximum(n, 1) / max_exact) / np.log(T5_MAX_DIST / max_exact)
                         * (nb - max_exact)).astype(np.int32)
    large = np.minimum(large, nb - 1)
    return (ret + np.where(n < max_exact, n, large)).astype(np.int32)


def _dilated_bias(t5_table, dilation):
    kw = A_BLOCK + 2 * A_RADIUS
    rel = np.arange(kw)[:, None] - A_RADIUS - np.arange(A_BLOCK)[None, :]
    bias = jnp.transpose(t5_table).astype(F32)[:, _t5_bucket(rel * dilation)]
    band = np.abs(rel) <= A_RADIUS
    variants = []
    for v in range(4):
        row_ok = np.ones((kw, 1), bool)
        if v & 1:
            row_ok[:A_RADIUS] = False
        if v & 2:
            row_ok[kw - A_RADIUS:] = False
        masked = jnp.where((band & row_ok)[None], bias * LOG2E, NEG)
        variants.append(masked.reshape(A_HEADS // 2, 2, kw, A_BLOCK).transpose(0, 2, 1, 3)
                        .reshape(A_HEADS // 2, kw, 2 * A_BLOCK))
    return jnp.stack(variants, axis=0)


def _dilated_kernel(q_ref, kvp_ref, kvc_ref, kvn_ref, bias_ref, o_ref, lse_ref, kv_buf, vt_buf, *, chunk, sub_len, group):
    kw = A_BLOCK + 2 * A_RADIUS
    pair_w = 2 * HEAD_DIM
    n_pairs = A_HEADS // 2
    kv_buf[:, 0:A_RADIUS] = kvp_ref[...]
    kv_buf[:, A_RADIUS:A_RADIUS + chunk] = kvc_ref[...]
    kv_buf[:, A_RADIUS + chunk:] = kvn_ref[...]
    for rl in range(group):
        vt_buf[rl] = kv_buf[rl, :, A_WIDTH:2 * A_WIDTH].T
    chunk_start = pl.program_id(2) * chunk
    first_head = lax.broadcasted_iota(jnp.int32, (A_BLOCK, pair_w), 1) < HEAD_DIM
    items = [(rl, j, jp) for rl in range(group) for j in range(chunk // A_BLOCK) for jp in range(n_pairs)]

    def scores(rl, j, jp):
        row0 = j * A_BLOCK
        pos = chunk_start + row0
        variant = (pos == 0).astype(jnp.int32) + 2 * (pos + A_BLOCK == sub_len).astype(jnp.int32)
        cols = slice(jp * pair_w, (jp + 1) * pair_w)
        qp = q_ref[rl, row0:row0 + A_BLOCK, cols].astype(F32)
        qbd = jnp.concatenate([jnp.where(first_head, qp, 0.0), jnp.where(first_head, 0.0, qp)],
                              axis=0).astype(BF16)
        kp = kv_buf[rl, row0:row0 + kw, cols]
        s = lax.dot_general(kp, qbd, (((1,), (1,)), ((), ())), preferred_element_type=F32)
        return s + bias_ref[variant, jp]

    pending = [scores(*item) for item in items[:A_LOOKAHEAD]]
    lse_rows = []
    for idx, (rl, j, jp) in enumerate(items):
        row0 = j * A_BLOCK
        cols = slice(jp * pair_w, (jp + 1) * pair_w)
        v_t = vt_buf[rl, jp * pair_w:(jp + 1) * pair_w, row0:row0 + kw]
        s = pending.pop(0)
        if idx + A_LOOKAHEAD < len(items):
            pending.append(scores(*items[idx + A_LOOKAHEAD]))
        m = jnp.max(s, axis=0, keepdims=True)
        p = jnp.exp2(s - m)
        den = jnp.sum(p, axis=0, keepdims=True)
        res = jnp.dot(v_t, p.astype(BF16), preferred_element_type=F32)
        res = res * (1.0 / den)
        o_t = jnp.concatenate([res[0:HEAD_DIM, 0:A_BLOCK], res[HEAD_DIM:pair_w, A_BLOCK:2 * A_BLOCK]], axis=0)
        o_ref[rl, row0:row0 + A_BLOCK, cols] = o_t.T.astype(BF16)
        lse = (m + jnp.log2(den)) * LN2
        lse_rows += [lse[:, 0:A_BLOCK], lse[:, A_BLOCK:2 * A_BLOCK]]
        if jp == n_pairs - 1:
            lse_t = jnp.concatenate(lse_rows + [jnp.zeros((LSE_LANES - A_HEADS, A_BLOCK), F32)], axis=0)
            lse_ref[rl, row0:row0 + A_BLOCK, :] = lse_t.T
            lse_rows = []


def _dilated_branch(zd, bias):
    b, dilation, sub_len, _ = zd.shape
    chunk = min(A_CHUNK, sub_len)
    group = min(dilation, A_CHUNK // chunk)
    halo_blocks = sub_len // A_RADIUS
    per_chunk = chunk // A_RADIUS
    kw = A_BLOCK + 2 * A_RADIUS
    kv_w = 2 * A_WIDTH
    return pl.pallas_call(
        functools.partial(_dilated_kernel, chunk=chunk, sub_len=sub_len, group=group),
        grid=(b, dilation // group, sub_len // chunk),
        in_specs=[
            pl.BlockSpec((None, group, chunk, A_WIDTH), lambda i, r, c: (i, r, c, 2)),
            pl.BlockSpec((None, group, A_RADIUS, kv_w),
                         lambda i, r, c: (i, r, jnp.maximum(c * per_chunk - 1, 0), 0)),
            pl.BlockSpec((None, group, chunk, kv_w), lambda i, r, c: (i, r, c, 0)),
            pl.BlockSpec((None, group, A_RADIUS, kv_w),
                         lambda i, r, c: (i, r, jnp.minimum((c + 1) * per_chunk, halo_blocks - 1), 0)),
            _const_spec((4, A_HEADS // 2, kw, 2 * A_BLOCK)),
        ],
        out_specs=[pl.BlockSpec((None, group, chunk, A_WIDTH), lambda i, r, c: (i, r, c, 0)),
                   pl.BlockSpec((None, group, chunk, LSE_LANES), lambda i, r, c: (i, r, c, 0))],
        out_shape=[jax.ShapeDtypeStruct((b, dilation, sub_len, A_WIDTH), BF16),
                   jax.ShapeDtypeStruct((b, dilation, sub_len, LSE_LANES), F32)],
        scratch_shapes=[pltpu.VMEM((group, chunk + 2 * A_RADIUS, kv_w), BF16),
                        pltpu.VMEM((group, A_WIDTH, chunk + 2 * A_RADIUS), BF16)],
        compiler_params=_params(3),
        name=f"dilated_d{dilation}",
    )(zd, zd, zd, zd, bias)


def _natten_bias(rpb):
    c = np.arange(GRID_W)
    cstart = np.clip(c - NA_COLS // 2, 0, GRID_W - NA_COLS)
    colmask = (c[None, :] >= cstart[:, None]) & (c[None, :] < cstart[:, None] + NA_COLS)
    coloff = np.clip(c[None, :] - c[:, None] + NA_COLS - 1, 0, 2 * NA_COLS - 2)
    bias = jnp.where(colmask[None, None], rpb[:, :, coloff].astype(F32) * LOG2E, NEG)
    bias = jnp.transpose(bias, (0, 1, 3, 2))
    bias = bias.reshape(C_HEADS // NA_QUAD, NA_QUAD, NA_ROWOFFS, GRID_W, GRID_W)
    return jnp.transpose(bias, (0, 2, 3, 1, 4)).reshape(C_HEADS // NA_QUAD, NA_ROWOFFS, GRID_W, NA_QUAD * GRID_W)


def _natten_kernel(q_ref, kp_ref, kc_ref, kn_ref, vp_ref, vc_ref, vn_ref, bias_ref, o_ref, k_buf, v_buf,
                   *, grid_rows):
    halo = (NA_ROWS // 2) * GRID_W
    chunk = NA_CHUNK_ROWS * GRID_W
    nkeys = NA_ROWS * GRID_W
    quad_w = NA_QUAD * HEAD_DIM
    pair_w = 2 * HEAD_DIM
    k_buf[0:halo] = kp_ref[...]
    k_buf[halo:halo + chunk] = kc_ref[...]
    k_buf[halo + chunk:] = kn_ref[...]
    v_buf[0:halo] = vp_ref[...]
    v_buf[halo:halo + chunk] = vc_ref[...]
    v_buf[halo + chunk:] = vn_ref[...]
    row_base = pl.program_id(1) * NA_CHUNK_ROWS
    lane_head = lax.broadcasted_iota(jnp.int32, (GRID_W, quad_w), 1) // HEAD_DIM
    first_head = lax.broadcasted_iota(jnp.int32, (GRID_W, pair_w), 1) < HEAD_DIM

    def window(rr):
        r = row_base + rr
        rs = jnp.clip(r - NA_ROWS // 2, 0, grid_rows - NA_ROWS)
        return rs - r + (NA_ROWS - 1), pl.multiple_of((rs - (row_base - NA_ROWS // 2)) * GRID_W, GRID_W)

    windows = [window(rr) for rr in range(NA_CHUNK_ROWS)]
    items = [(rr, g) for rr in range(NA_CHUNK_ROWS) for g in range(C_HEADS // NA_QUAD)]

    def scores(rr, g):
        ro0, k0 = windows[rr]
        cols = slice(g * quad_w, (g + 1) * quad_w)
        qs = q_ref[rr * GRID_W:(rr + 1) * GRID_W, cols].astype(F32)
        qbd = jnp.concatenate([jnp.where(lane_head == hh, qs, 0.0) for hh in range(NA_QUAD)],
                              axis=0).astype(BF16)
        kq = k_buf[pl.ds(k0, nkeys), cols]
        s = lax.dot_general(kq, qbd, (((1,), (1,)), ((), ())), preferred_element_type=F32)
        return jnp.concatenate([s[i * GRID_W:(i + 1) * GRID_W] + bias_ref[g, ro0 + i]
                                for i in range(NA_ROWS)], axis=0)

    pending = [scores(*item) for item in items[:NA_LOOKAHEAD]]
    for idx, (rr, g) in enumerate(items):
        cols = slice(g * quad_w, (g + 1) * quad_w)
        vq = v_buf[pl.ds(windows[rr][1], nkeys), cols]
        s = pending.pop(0)
        if idx + NA_LOOKAHEAD < len(items):
            pending.append(scores(*items[idx + NA_LOOKAHEAD]))
        m = jnp.max(s, axis=0, keepdims=True)
        p = jnp.exp2(s - m)
        den = jnp.sum(p, axis=0, keepdims=True)
        res = lax.dot_general(vq, p.astype(BF16), (((0,), (0,)), ((), ())), preferred_element_type=F32)
        res = res * (1.0 / den)
        slabs = []
        for half in range(NA_QUAD // 2):
            blk = res[half * pair_w:(half + 1) * pair_w, half * pair_w:(half + 1) * pair_w].T
            slabs.append(jnp.where(first_head, blk[0:GRID_W], blk[GRID_W:2 * GRID_W]))
        o_ref[rr * GRID_W:(rr + 1) * GRID_W, cols] = jnp.concatenate(slabs, axis=-1).astype(BF16)


def _natten(z, bias):
    b, l, _ = z.shape
    grid_rows = l // GRID_W
    chunk = NA_CHUNK_ROWS * GRID_W
    halo = (NA_ROWS // 2) * GRID_W
    halo_blocks = l // halo
    per_chunk = chunk // halo

    def cur(col):
        return pl.BlockSpec((None, chunk, C_WIDTH), lambda i, c: (i, c, col))

    def prev(col):
        return pl.BlockSpec((None, halo, C_WIDTH), lambda i, c: (i, jnp.maximum(c * per_chunk - 1, 0), col))

    def nxt(col):
        return pl.BlockSpec((None, halo, C_WIDTH),
                            lambda i, c: (i, jnp.minimum((c + 1) * per_chunk, halo_blocks - 1), col))

    return pl.pallas_call(
        functools.partial(_natten_kernel, grid_rows=grid_rows),
        grid=(b, l // chunk),
        in_specs=[cur(0), prev(1), cur(1), nxt(1), prev(2), cur(2), nxt(2),
                  _const_spec((C_HEADS // NA_QUAD, NA_ROWOFFS, GRID_W, NA_QUAD * GRID_W))],
        out_specs=pl.BlockSpec((None, chunk, C_WIDTH), lambda i, c: (i, c, 0)),
        out_shape=jax.ShapeDtypeStruct((b, l, C_WIDTH), BF16),
        scratch_shapes=[pltpu.VMEM((chunk + 2 * halo, C_WIDTH), BF16),
                        pltpu.VMEM((chunk + 2 * halo, C_WIDTH), BF16)],
        compiler_params=_params(2),
        name="natten",
    )(z, z, z, z, z, z, z, bias)


def _ffn_tail(x_ref, mix, wo_ref, g_ref, gate1_ref, sh2_ref, sc2_ref, gate2_ref, wg_ref, wu_ref, wd_ref, o_ref):
    rows = o_ref.shape[0] // TAIL_SUBTILES
    groups = [slice(i * rows, (i + 1) * rows) for i in range(TAIL_SUBTILES)]
    ys = [jnp.dot(mix[r, :], wo_ref[...], preferred_element_type=F32) for r in groups]
    x1s = [x_ref[r, :] + gate1_ref[...] * _rms(y, g_ref[0:1]) for r, y in zip(groups, ys)]
    hs = [(_rms(x1, g_ref[1:2]) * (1.0 + sc2_ref[...]) + sh2_ref[...]).astype(BF16) for x1 in x1s]
    d_ff = wg_ref.shape[1]
    bounds = [min(c * FFN_CHUNK, d_ff) for c in range(-(-d_ff // FFN_CHUNK) + 1)]
    y2s = None
    for lo, hi in zip(bounds[:-1], bounds[1:]):
        cols = slice(lo, hi)
        gates = [jnp.dot(h, wg_ref[:, cols], preferred_element_type=F32) for h in hs]
        ups = [jnp.dot(h, wu_ref[:, cols], preferred_element_type=F32) for h in hs]
        acts = [(gate * jax.nn.sigmoid(gate) * up).astype(BF16) for gate, up in zip(gates, ups)]
        parts = [jnp.dot(act, wd_ref[cols, :], preferred_element_type=F32) for act in acts]
        y2s = parts if y2s is None else [y2 + part for y2, part in zip(y2s, parts)]
    for r, x1, y2 in zip(groups, x1s, y2s):
        o_ref[r, :] = x1 + gate2_ref[...] * _rms(y2, g_ref[2:3])


def _token_order(src_ref, buf, dilation, tm):
    n_slabs = buf.shape[0]
    for rho in range(dilation):
        rows = src_ref[rho].astype(F32)
        for cb in range(n_slabs):
            buf[cb, pl.ds(rho, tm // dilation, stride=dilation), :] = rows[:, cb * LANES:(cb + 1) * LANES]
    return jnp.concatenate([buf[cb] for cb in range(n_slabs)], axis=-1)


def _post_even_kernel(x_ref, o1_ref, o4_ref, o16_ref, l1_ref, l4_ref, l16_ref, up_ref, uc_ref, un_ref,
                      expand_ref, pw_ref, ps_ref, wo_ref, g_ref, gate1_ref, sh2_ref, sc2_ref, gate2_ref,
                      wg_ref, wu_ref, wd_ref, o_ref, u_buf, o4_buf, o16_buf, l4_buf, l16_buf, *, tm, seq_len):
    outs = (o1_ref[...].astype(F32), _token_order(o4_ref, o4_buf, 4, tm), _token_order(o16_ref, o16_buf, 16, tm))
    l1, l2, l3 = l1_ref[...], _token_order(l4_ref, l4_buf, 4, tm), _token_order(l16_ref, l16_buf, 16, tm)
    mx = jnp.maximum(jnp.maximum(l1, l2), l3)
    es = (jnp.exp(l1 - mx), jnp.exp(l2 - mx), jnp.exp(l3 - mx))
    inv = 1.0 / (es[0] + es[1] + es[2])
    ya = jnp.zeros((tm, A_WIDTH), F32)
    for e, o_br in zip(es, outs):
        wts = jnp.dot((e * inv).astype(BF16), expand_ref[...], preferred_element_type=F32)
        ya = ya + wts * o_br

    t0 = pl.program_id(1) * tm
    u_buf[0:B_HALO] = jnp.where(t0 > 0, up_ref[...].astype(F32), 0.0)
    u_buf[B_HALO:B_HALO + tm] = uc_ref[...].astype(F32)
    u_buf[B_HALO + tm:] = jnp.where(t0 + tm < seq_len, un_ref[...].astype(F32), 0.0)
    t = t0 + lax.broadcasted_iota(jnp.int32, (tm, 1), 0)
    yb = []
    for g, w in enumerate(B_WINDOWS):
        cols = slice(g * B_GROUP_DIM, (g + 1) * B_GROUP_DIM)
        total = jnp.zeros((tm, B_GROUP_DIM), F32)
        for j in range(-(w // 2), w // 2):
            total = total + u_buf[B_HALO + j:B_HALO + j + tm, cols]
        cnt = (jnp.clip(t + w // 2, 0, seq_len) - jnp.clip(t - w // 2, 0, seq_len)).astype(F32)
        pooled = total / cnt - u_buf[B_HALO:B_HALO + tm, cols]
        yg = jnp.dot(pooled.astype(BF16), pw_ref[g], preferred_element_type=F32)
        yb.append(yg * ps_ref[:, cols])
    mix = jnp.concatenate([ya] + yb, axis=-1).astype(BF16)
    _ffn_tail(x_ref, mix, wo_ref, g_ref, gate1_ref, sh2_ref, sc2_ref, gate2_ref, wg_ref, wu_ref, wd_ref, o_ref)


def _post_odd_kernel(x_ref, mix_ref, wo_ref, g_ref, gate1_ref, sh2_ref, sc2_ref, gate2_ref,
                     wg_ref, wu_ref, wd_ref, o_ref):
    _ffn_tail(x_ref, mix_ref, wo_ref, g_ref, gate1_ref, sh2_ref, sc2_ref, gate2_ref, wg_ref, wu_ref, wd_ref, o_ref)


def _tail_specs(d, d_ff):
    vec = pl.BlockSpec((None, 1, d), lambda i, t: (i, 0, 0))
    return [_const_spec((d, d)), _const_spec((3, d)), vec, vec, vec, vec,
            _const_spec((d, d_ff)), _const_spec((d, d_ff)), _const_spec((d_ff, d))]


def _post_even(x, z1, branches, expand, pool_w, pool_scale, w_out, g3, gate1, sh2, sc2, gate2, wg, wu, wd, tm):
    b, l, d = x.shape
    d_ff = wg.shape[1]
    (o1, l1), (o4, l4), (o16, l16) = branches
    halo_blocks = l // B_HALO
    per_tile = tm // B_HALO
    tok = lambda w: pl.BlockSpec((None, tm, w), lambda i, t: (i, t, 0))
    grouped = lambda dil, w: pl.BlockSpec((None, dil, tm // dil, w), lambda i, t: (i, 0, t, 0))
    u_col = 3
    in_specs = [tok(d), tok(A_WIDTH), grouped(4, A_WIDTH), grouped(16, A_WIDTH),
                tok(LSE_LANES), grouped(4, LSE_LANES), grouped(16, LSE_LANES),
                pl.BlockSpec((None, B_HALO, B_WIDTH), lambda i, t: (i, jnp.maximum(t * per_tile - 1, 0), u_col)),
                pl.BlockSpec((None, tm, B_WIDTH), lambda i, t: (i, t, u_col)),
                pl.BlockSpec((None, B_HALO, B_WIDTH),
                             lambda i, t: (i, jnp.minimum((t + 1) * per_tile, halo_blocks - 1), u_col)),
                _const_spec((LSE_LANES, A_WIDTH)), _const_spec((B_GROUPS, B_GROUP_DIM, B_GROUP_DIM)),
                _const_spec((1, B_WIDTH))] + _tail_specs(d, d_ff)
    return pl.pallas_call(
        functools.partial(_post_even_kernel, tm=tm, seq_len=l),
        grid=(b, l // tm),
        in_specs=in_specs,
        out_specs=tok(d),
        out_shape=jax.ShapeDtypeStruct((b, l, d), F32),
        scratch_shapes=[pltpu.VMEM((tm + 2 * B_HALO, B_WIDTH), F32),
                        pltpu.VMEM((A_WIDTH // LANES, tm, LANES), F32),
                        pltpu.VMEM((A_WIDTH // LANES, tm, LANES), F32),
                        pltpu.VMEM((LSE_LANES // LANES, tm, LANES), F32),
                        pltpu.VMEM((LSE_LANES // LANES, tm, LANES), F32)],
        compiler_params=_params(2),
        name="post_even",
    )(x, o1.reshape(b, l, A_WIDTH), o4, o16, l1.reshape(b, l, LSE_LANES), l4, l16, z1, z1, z1,
      expand, pool_w, pool_scale.reshape(1, B_WIDTH), w_out, g3, gate1, sh2, sc2, gate2, wg, wu, wd)


def _post_odd(x, mix, w_out, g3, gate1, sh2, sc2, gate2, wg, wu, wd, tm):
    b, l, d = x.shape
    d_ff = wg.shape[1]
    tok = lambda w: pl.BlockSpec((None, tm, w), lambda i, t: (i, t, 0))
    return pl.pallas_call(
        _post_odd_kernel,
        grid=(b, l // tm),
        in_specs=[tok(d), tok(C_WIDTH)] + _tail_specs(d, d_ff),
        out_specs=tok(d),
        out_shape=jax.ShapeDtypeStruct((b, l, d), F32),
        compiler_params=_params(2),
        name="post_odd",
    )(x, mix, w_out, g3, gate1, sh2, sc2, gate2, wg, wu, wd)


def _head_expand_matrix():
    rows = np.arange(LSE_LANES)[:, None]
    cols = np.arange(A_WIDTH)[None, :]
    return jnp.asarray(rows == cols // HEAD_DIM, BF16)


def kernel(x_prompt, x_sample, c_prompt, c_sample, norm_g, ada_w, ada_b, ffn_w_gate, ffn_w_up, ffn_w_down,
           even_w_in, even_w_out, pool_w, pool_scale, t5_table, odd_w_qkv, odd_w_out, odd_rpb):
    depth = norm_g.shape[0]
    d = x_prompt.shape[-1]
    nb_prompt = c_prompt.shape[0]
    mod_all = _ada(jnp.concatenate([c_prompt, c_sample], axis=0), ada_w, ada_b)
    expand = _head_expand_matrix()
    biases = [_dilated_bias(t5_table, dil) for dil in A_DILATIONS]

    layers = []
    for layer in range(depth):
        i = layer // 2
        w = dict(ffn=(ffn_w_gate[layer].astype(BF16), ffn_w_up[layer].astype(BF16), ffn_w_down[layer].astype(BF16)))
        if layer % 2 == 0:
            wq, wk, wv, wu = jnp.split(even_w_in[i], [A_WIDTH, 2 * A_WIDTH, 3 * A_WIDTH], axis=-1)
            w.update(w_in=jnp.concatenate([wk, wv, wq * Q_SCALE, wu], axis=-1).astype(BF16),
                     w_out=even_w_out[i].astype(BF16), pool_w=pool_w[i].astype(BF16), pool_scale=pool_scale[i])
        else:
            wq, wk, wv = jnp.split(odd_w_qkv[i], 3, axis=-1)
            w.update(w_in=jnp.concatenate([wq * Q_SCALE, wk, wv], axis=-1).astype(BF16),
                     w_out=odd_w_out[i].astype(BF16), rpb=_natten_bias(odd_rpb[i]))
        layers.append(w)

    def trunk(x, mod, tm):
        b, l, _ = x.shape
        for layer, w in enumerate(layers):
            sh1, sc1, g1, sh2, sc2, g2 = [m.reshape(b, 1, d) for m in jnp.split(mod[layer], 6, axis=-1)]
            g = norm_g[layer]
            if layer % 2 == 0:
                z1, z4, z16 = _pre_even(x, g[0], sh1, sc1, w["w_in"], tm)
                branches = [_dilated_branch(zd, bias)
                            for zd, bias in zip((z1.reshape(b, 1, l, EVEN_IN), z4, z16), biases)]
                x = _post_even(x, z1, branches, expand, w["pool_w"], w["pool_scale"], w["w_out"],
                               g[1:4], g1, sh2, sc2, g2, *w["ffn"], tm)
            else:
                z = _pre_odd(x, g[0], sh1, sc1, w["w_in"], tm)
                mix = _natten(z, w["rpb"])
                x = _post_odd(x, mix, w["w_out"], g[1:4], g1, sh2, sc2, g2, *w["ffn"], tm)
        return x

    y_prompt = trunk(x_prompt, mod_all[:, :nb_prompt], TOKEN_TILE)
    y_sample = trunk(x_sample, mod_all[:, nb_prompt:], TOKEN_TILE)
    return (y_prompt, y_sample)
```

```python
import functools

import numpy as np
import jax
import jax.numpy as jnp
from jax import lax
from jax.experimental import pallas as pl
from jax.experimental.pallas import tpu as pltpu

D_MODEL = 1024
HEAD_DIM = 64
ATTN_SCALE = HEAD_DIM ** -0.5
LOG2E = float(np.log2(np.e))
LN2 = float(np.log(2.0))
Q_SCALE = ATTN_SCALE * LOG2E
A_HEADS = 8
A_WIDTH = A_HEADS * HEAD_DIM
A_DILATIONS = (1, 4, 16)
A_BLOCK = 128
A_RADIUS = 64
A_CHUNK = 512
A_LOOKAHEAD = 4
B_GROUPS = 4
B_WINDOWS = (2, 4, 8, 16)
B_WIDTH = 512
B_GROUP_DIM = 128
B_HALO = 16
B_PAD = 64
EVEN_IN = 3 * A_WIDTH + B_WIDTH
C_HEADS = 16
C_WIDTH = C_HEADS * HEAD_DIM
GRID_W = 64
NA_ROWS = 8
NA_COLS = 16
NA_CHUNK_ROWS = 8
NA_QUAD = 4
NA_ROWOFFS = 2 * NA_ROWS - 1
NA_LOOKAHEAD = 2
T5_BUCKETS = 32
T5_MAX_DIST = 1024
NEG = -1e30
EPS = 1e-6
LANES = 128
LSE_LANES = 128
TOKEN_TILE = 512
TAIL_SUBTILES = 2
PRE_SUBTILES = 2
PRE_COL_CHUNK = 1024
FFN_CHUNK = 1536
V7X_VMEM_LIMIT_BYTES = 60 * 1024 * 1024

BF16 = jnp.bfloat16
F32 = jnp.float32


def _const_spec(shape):
    zeros = (0,) * len(shape)
    return pl.BlockSpec(shape, lambda *_: zeros, pipeline_mode=pl.Buffered(1))


def _params(n_axes):
    return pltpu.CompilerParams(dimension_semantics=("arbitrary",) * n_axes,
                                vmem_limit_bytes=V7X_VMEM_LIMIT_BYTES)


def _rms(x, g):
    return x * lax.rsqrt(jnp.mean(x * x, axis=-1, keepdims=True) + EPS) * g


def _ada_kernel(c_ref, w_ref, b_ref, o_ref):
    c = c_ref[...]
    s = (c * jax.nn.sigmoid(c)).astype(BF16)
    o_ref[...] = jnp.dot(s, w_ref[...].astype(BF16), preferred_element_type=F32) + b_ref[...]


def _ada(c_all, ada_w, ada_b):
    depth, d, six_d = ada_w.shape
    nb = c_all.shape[0]
    return pl.pallas_call(
        _ada_kernel,
        grid=(depth, six_d // d),
        in_specs=[pl.BlockSpec((nb, d), lambda l, j: (0, 0)),
                  pl.BlockSpec((None, d, d), lambda l, j: (l, 0, j)),
                  pl.BlockSpec((None, 1, d), lambda l, j: (l, 0, j))],
        out_specs=pl.BlockSpec((None, nb, d), lambda l, j: (l, 0, j)),
        out_shape=jax.ShapeDtypeStruct((depth, nb, six_d), F32),
        compiler_params=_params(2),
        name="ada_mod",
    )(c_all, ada_w, ada_b.reshape(depth, 1, six_d))


def _modulated(x_ref, g_ref, sh_ref, sc_ref):
    return (_rms(x_ref[...], g_ref[...]) * (1.0 + sc_ref[...]) + sh_ref[...]).astype(BF16)


def _pre_odd_kernel(x_ref, g_ref, sh_ref, sc_ref, w_ref, o_ref, *, col_chunk):
    hb = _modulated(x_ref, g_ref, sh_ref, sc_ref)
    for j in range(w_ref.shape[1] // col_chunk):
        cols = slice(j * col_chunk, (j + 1) * col_chunk)
        o_ref[:, cols] = jnp.dot(hb, w_ref[:, cols], preferred_element_type=F32).astype(BF16)


def _pre_even_kernel(x_ref, g_ref, sh_ref, sc_ref, w_ref, z1_ref, z4_ref, z16_ref, *zs_bufs, col_chunk, tm):
    per_chunk = col_chunk // LANES
    rows = tm // PRE_SUBTILES
    n_chunks = w_ref.shape[1] // col_chunk
    items = [(sub, j) for sub in range(PRE_SUBTILES) for j in range(n_chunks)]

    def regroup(sub, j):
        zs = zs_bufs[sub * n_chunks + j]
        for cb in range(min(per_chunk, 3 * A_WIDTH // LANES - j * per_chunk)):
            out_cols = slice((j * per_chunk + cb) * LANES, (j * per_chunk + cb + 1) * LANES)
            for dil, ref in ((4, z4_ref), (16, z16_ref)):
                n = rows // dil
                for rho in range(dil):
                    ref[rho, sub * n:(sub + 1) * n, out_cols] = zs[cb, pl.ds(rho, n, stride=dil), :].astype(BF16)

    previous = None
    hbs = {}
    for sub, j in items:
        r = slice(sub * rows, (sub + 1) * rows)
        cols = slice(j * col_chunk, (j + 1) * col_chunk)
        if sub not in hbs:
            hbs[sub] = (_rms(x_ref[r, :], g_ref[...]) * (1.0 + sc_ref[...]) + sh_ref[...]).astype(BF16)
        z = jnp.dot(hbs[sub], w_ref[:, cols], preferred_element_type=F32)
        z1_ref[r, cols] = z.astype(BF16)
        for cb in range(per_chunk):
            zs_bufs[sub * n_chunks + j][cb] = z[:, cb * LANES:(cb + 1) * LANES]
        if previous is not None:
            regroup(*previous)
        previous = (sub, j)
    regroup(*previous)


def _pre_specs(d, n, tm):
    vec = pl.BlockSpec((None, 1, d), lambda i, t: (i, 0, 0))
    return [pl.BlockSpec((None, tm, d), lambda i, t: (i, t, 0)), _const_spec((1, d)), vec, vec,
            _const_spec((d, n))]


def _pre_odd(x, g, shift, scale, w, tm):
    b, l, d = x.shape
    n = w.shape[1]
    return pl.pallas_call(
        functools.partial(_pre_odd_kernel, col_chunk=PRE_COL_CHUNK),
        grid=(b, l // tm),
        in_specs=_pre_specs(d, n, tm),
        out_specs=pl.BlockSpec((None, tm, n), lambda i, t: (i, t, 0)),
        out_shape=jax.ShapeDtypeStruct((b, l, n), BF16),
        compiler_params=_params(2),
        name="pre_odd",
    )(x, g.reshape(1, d), shift, scale, w)


def _pre_even(x, g, shift, scale, w, tm):
    b, l, d = x.shape
    n = w.shape[1]
    qkv = 3 * A_WIDTH
    return pl.pallas_call(
        functools.partial(_pre_even_kernel, col_chunk=PRE_COL_CHUNK, tm=tm),
        grid=(b, l // tm),
        in_specs=_pre_specs(d, n, tm),
        out_specs=[pl.BlockSpec((None, tm, n), lambda i, t: (i, t, 0)),
                   pl.BlockSpec((None, 4, tm // 4, qkv), lambda i, t: (i, 0, t, 0)),
                   pl.BlockSpec((None, 16, tm // 16, qkv), lambda i, t: (i, 0, t, 0))],
        out_shape=[jax.ShapeDtypeStruct((b, l, n), BF16),
                   jax.ShapeDtypeStruct((b, 4, l // 4, qkv), BF16),
                   jax.ShapeDtypeStruct((b, 16, l // 16, qkv), BF16)],
        scratch_shapes=[pltpu.VMEM((PRE_COL_CHUNK // LANES, tm // PRE_SUBTILES, LANES), F32)
                        for _ in range(PRE_SUBTILES * (n // PRE_COL_CHUNK))],
        compiler_params=_params(2),
        name="pre_even",
    )(x, g.reshape(1, d), shift, scale, w)


def _t5_bucket(rel):
    nb = T5_BUCKETS // 2
    max_exact = nb // 2
    ret = (rel > 0).astype(np.int32) * nb
    n = np.abs(rel)
    large = max_exact + (np.log(np.maximum(n, 1) / max_exact) / np.log(T5_MAX_DIST / max_exact)
                         * (nb - max_exact)).astype(np.int32)
    large = np.minimum(large, nb - 1)
    return (ret + np.where(n < max_exact, n, large)).astype(np.int32)


def _lookup_last(table, idx):
    n = table.shape[-1]
    onehot = jnp.asarray(idx)[None] == jnp.arange(n, dtype=jnp.int32).reshape((n,) + (1,) * idx.ndim)
    expanded = table.astype(F32).reshape(table.shape + (1,) * idx.ndim)
    return jnp.sum(jnp.where(onehot, expanded, 0.0), axis=table.ndim - 1)


def _dilated_bias(t5_table, dilation):
    kw = A_BLOCK + 2 * A_RADIUS
    rel = np.arange(kw)[:, None] - A_RADIUS - np.arange(A_BLOCK)[None, :]
    bias = _lookup_last(jnp.transpose(t5_table), _t5_bucket(rel * dilation))
    band = np.abs(rel) <= A_RADIUS
    variants = []
    for v in range(4):
        row_ok = np.ones((kw, 1), bool)
        if v & 1:
            row_ok[:A_RADIUS] = False
        if v & 2:
            row_ok[kw - A_RADIUS:] = False
        masked = jnp.where((band & row_ok)[None], bias * LOG2E, NEG)
        variants.append(masked.reshape(A_HEADS // 2, 2, kw, A_BLOCK).transpose(0, 2, 1, 3)
                        .reshape(A_HEADS // 2, kw, 2 * A_BLOCK))
    return jnp.stack(variants, axis=0)


def _dilated_kernel(q_ref, kvp_ref, kvc_ref, kvn_ref, bias_ref, o_ref, lse_ref, kv_buf, vt_buf, *, chunk, sub_len, group):
    kw = A_BLOCK + 2 * A_RADIUS
    pair_w = 2 * HEAD_DIM
    n_pairs = A_HEADS // 2
    kv_buf[:, 0:A_RADIUS] = kvp_ref[...]
    kv_buf[:, A_RADIUS:A_RADIUS + chunk] = kvc_ref[...]
    kv_buf[:, A_RADIUS + chunk:] = kvn_ref[...]
    for rl in range(group):
        vt_buf[rl] = kv_buf[rl, :, A_WIDTH:2 * A_WIDTH].T
    chunk_start = pl.program_id(2) * chunk
    first_head = lax.broadcasted_iota(jnp.int32, (A_BLOCK, pair_w), 1) < HEAD_DIM
    items = [(rl, j, jp) for rl in range(group) for j in range(chunk // A_BLOCK) for jp in range(n_pairs)]

    def scores(rl, j, jp):
        row0 = j * A_BLOCK
        pos = chunk_start + row0
        variant = (pos == 0).astype(jnp.int32) + 2 * (pos + A_BLOCK == sub_len).astype(jnp.int32)
        cols = slice(jp * pair_w, (jp + 1) * pair_w)
        qp = q_ref[rl, row0:row0 + A_BLOCK, cols].astype(F32)
        qbd = jnp.concatenate([jnp.where(first_head, qp, 0.0), jnp.where(first_head, 0.0, qp)],
                              axis=0).astype(BF16)
        kp = kv_buf[rl, row0:row0 + kw, cols]
        s = lax.dot_general(kp, qbd, (((1,), (1,)), ((), ())), preferred_element_type=F32)
        return s + bias_ref[variant, jp]

    pending = [scores(*item) for item in items[:A_LOOKAHEAD]]
    lse_rows = []
    for idx, (rl, j, jp) in enumerate(items):
        row0 = j * A_BLOCK
        cols = slice(jp * pair_w, (jp + 1) * pair_w)
        v_t = vt_buf[rl, jp * pair_w:(jp + 1) * pair_w, row0:row0 + kw]
        s = pending.pop(0)
        if idx + A_LOOKAHEAD < len(items):
            pending.append(scores(*items[idx + A_LOOKAHEAD]))
        m = jnp.max(s, axis=0, keepdims=True)
        p = jnp.exp2(s - m)
        den = jnp.sum(p, axis=0, keepdims=True)
        res = jnp.dot(v_t, p.astype(BF16), preferred_element_type=F32)
        res = res * (1.0 / den)
        o_t = jnp.concatenate([res[0:HEAD_DIM, 0:A_BLOCK], res[HEAD_DIM:pair_w, A_BLOCK:2 * A_BLOCK]], axis=0)
        o_ref[rl, row0:row0 + A_BLOCK, cols] = o_t.T.astype(BF16)
        lse = (m + jnp.log2(den)) * LN2
        lse_rows += [lse[:, 0:A_BLOCK], lse[:, A_BLOCK:2 * A_BLOCK]]
        if jp == n_pairs - 1:
            lse_t = jnp.concatenate(lse_rows + [jnp.zeros((LSE_LANES - A_HEADS, A_BLOCK), F32)], axis=0)
            lse_ref[rl, row0:row0 + A_BLOCK, :] = lse_t.T
            lse_rows = []


def _dilated_branch(zd, bias):
    b, dilation, sub_len, _ = zd.shape
    chunk = min(A_CHUNK, sub_len)
    group = min(dilation, A_CHUNK // chunk)
    halo_blocks = sub_len // A_RADIUS
    per_chunk = chunk // A_RADIUS
    kw = A_BLOCK + 2 * A_RADIUS
    kv_w = 2 * A_WIDTH
    return pl.pallas_call(
        functools.partial(_dilated_kernel, chunk=chunk, sub_len=sub_len, group=group),
        grid=(b, dilation // group, sub_len // chunk),
        in_specs=[
            pl.BlockSpec((None, group, chunk, A_WIDTH), lambda i, r, c: (i, r, c, 2)),
            pl.BlockSpec((None, group, A_RADIUS, kv_w),
                         lambda i, r, c: (i, r, jnp.maximum(c * per_chunk - 1, 0), 0)),
            pl.BlockSpec((None, group, chunk, kv_w), lambda i, r, c: (i, r, c, 0)),
            pl.BlockSpec((None, group, A_RADIUS, kv_w),
                         lambda i, r, c: (i, r, jnp.minimum((c + 1) * per_chunk, halo_blocks - 1), 0)),
            _const_spec((4, A_HEADS // 2, kw, 2 * A_BLOCK)),
        ],
        out_specs=[pl.BlockSpec((None, group, chunk, A_WIDTH), lambda i, r, c: (i, r, c, 0)),
                   pl.BlockSpec((None, group, chunk, LSE_LANES), lambda i, r, c: (i, r, c, 0))],
        out_shape=[jax.ShapeDtypeStruct((b, dilation, sub_len, A_WIDTH), BF16),
                   jax.ShapeDtypeStruct((b, dilation, sub_len, LSE_LANES), F32)],
        scratch_shapes=[pltpu.VMEM((group, chunk + 2 * A_RADIUS, kv_w), BF16),
                        pltpu.VMEM((group, A_WIDTH, chunk + 2 * A_RADIUS), BF16)],
        compiler_params=_params(3),
        name=f"dilated_d{dilation}",
    )(zd, zd, zd, zd, bias)


def _natten_bias(rpb):
    c = np.arange(GRID_W)
    cstart = np.clip(c - NA_COLS // 2, 0, GRID_W - NA_COLS)
    colmask = (c[None, :] >= cstart[:, None]) & (c[None, :] < cstart[:, None] + NA_COLS)
    coloff = np.clip(c[None, :] - c[:, None] + NA_COLS - 1, 0, 2 * NA_COLS - 2)
    bias = jnp.where(colmask[None, None], _lookup_last(rpb, coloff.astype(np.int32)) * LOG2E, NEG)
    bias = jnp.transpose(bias, (0, 1, 3, 2))
    bias = bias.reshape(C_HEADS // NA_QUAD, NA_QUAD, NA_ROWOFFS, GRID_W, GRID_W)
    return jnp.transpose(bias, (0, 2, 3, 1, 4)).reshape(C_HEADS // NA_QUAD, NA_ROWOFFS, GRID_W, NA_QUAD * GRID_W)


def _natten_kernel(q_ref, kp_ref, kc_ref, kn_ref, vp_ref, vc_ref, vn_ref, bias_ref, o_ref, k_buf, v_buf,
                   *, grid_rows):
    halo = (NA_ROWS // 2) * GRID_W
    chunk = NA_CHUNK_ROWS * GRID_W
    nkeys = NA_ROWS * GRID_W
    quad_w = NA_QUAD * HEAD_DIM
    pair_w = 2 * HEAD_DIM
    k_buf[0:halo] = kp_ref[...]
    k_buf[halo:halo + chunk] = kc_ref[...]
    k_buf[halo + chunk:] = kn_ref[...]
    v_buf[0:halo] = vp_ref[...]
    v_buf[halo:halo + chunk] = vc_ref[...]
    v_buf[halo + chunk:] = vn_ref[...]
    row_base = pl.program_id(1) * NA_CHUNK_ROWS
    lane_head = lax.broadcasted_iota(jnp.int32, (GRID_W, quad_w), 1) // HEAD_DIM
    first_head = lax.broadcasted_iota(jnp.int32, (GRID_W, pair_w), 1) < HEAD_DIM

    def window(rr):
        r = row_base + rr
        rs = jnp.clip(r - NA_ROWS // 2, 0, grid_rows - NA_ROWS)
        return rs - r + (NA_ROWS - 1), pl.multiple_of((rs - (row_base - NA_ROWS // 2)) * GRID_W, GRID_W)

    windows = [window(rr) for rr in range(NA_CHUNK_ROWS)]
    items = [(rr, g) for rr in range(NA_CHUNK_ROWS) for g in range(C_HEADS // NA_QUAD)]

    def scores(rr, g):
        ro0, k0 = windows[rr]
        cols = slice(g * quad_w, (g + 1) * quad_w)
        qs = q_ref[rr * GRID_W:(rr + 1) * GRID_W, cols].astype(F32)
        qbd = jnp.concatenate([jnp.where(lane_head == hh, qs, 0.0) for hh in range(NA_QUAD)],
                              axis=0).astype(BF16)
        kq = k_buf[pl.ds(k0, nkeys), cols]
        s = lax.dot_general(kq, qbd, (((1,), (1,)), ((), ())), preferred_element_type=F32)
        return jnp.concatenate([s[i * GRID_W:(i + 1) * GRID_W] + bias_ref[g, ro0 + i]
                                for i in range(NA_ROWS)], axis=0)

    pending = [scores(*item) for item in items[:NA_LOOKAHEAD]]
    for idx, (rr, g) in enumerate(items):
        cols = slice(g * quad_w, (g + 1) * quad_w)
        vq = v_buf[pl.ds(windows[rr][1], nkeys), cols]
        s = pending.pop(0)
        if idx + NA_LOOKAHEAD < len(items):
            pending.append(scores(*items[idx + NA_LOOKAHEAD]))
        m = jnp.max(s, axis=0, keepdims=True)
        p = jnp.exp2(s - m)
        den = jnp.sum(p, axis=0, keepdims=True)
        res = lax.dot_general(vq, p.astype(BF16), (((0,), (0,)), ((), ())), preferred_element_type=F32)
        res = res * (1.0 / den)
        slabs = []
        for half in range(NA_QUAD // 2):
            blk = res[half * pair_w:(half + 1) * pair_w, half * pair_w:(half + 1) * pair_w].T
            slabs.append(jnp.where(first_head, blk[0:GRID_W], blk[GRID_W:2 * GRID_W]))
        o_ref[rr * GRID_W:(rr + 1) * GRID_W, cols] = jnp.concatenate(slabs, axis=-1).astype(BF16)


def _natten(z, bias):
    b, l, _ = z.shape
    grid_rows = l // GRID_W
    chunk = NA_CHUNK_ROWS * GRID_W
    halo = (NA_ROWS // 2) * GRID_W
    halo_blocks = l // halo
    per_chunk = chunk // halo

    def cur(col):
        return pl.BlockSpec((None, chunk, C_WIDTH), lambda i, c: (i, c, col))

    def prev(col):
        return pl.BlockSpec((None, halo, C_WIDTH), lambda i, c: (i, jnp.maximum(c * per_chunk - 1, 0), col))

    def nxt(col):
        return pl.BlockSpec((None, halo, C_WIDTH),
                            lambda i, c: (i, jnp.minimum((c + 1) * per_chunk, halo_blocks - 1), col))

    return pl.pallas_call(
        functools.partial(_natten_kernel, grid_rows=grid_rows),
        grid=(b, l // chunk),
        in_specs=[cur(0), prev(1), cur(1), nxt(1), prev(2), cur(2), nxt(2),
                  _const_spec((C_HEADS // NA_QUAD, NA_ROWOFFS, GRID_W, NA_QUAD * GRID_W))],
        out_specs=pl.BlockSpec((None, chunk, C_WIDTH), lambda i, c: (i, c, 0)),
        out_shape=jax.ShapeDtypeStruct((b, l, C_WIDTH), BF16),
        scratch_shapes=[pltpu.VMEM((chunk + 2 * halo, C_WIDTH), BF16),
                        pltpu.VMEM((chunk + 2 * halo, C_WIDTH), BF16)],
        compiler_params=_params(2),
        name="natten",
    )(z, z, z, z, z, z, z, bias)


def _ffn_tail(x_ref, mix_of_group, wo_ref, g_ref, gate1_ref, sh2_ref, sc2_ref, gate2_ref, wg_ref, wu_ref, wd_ref,
              o_ref):
    rows = o_ref.shape[0] // TAIL_SUBTILES
    groups = [slice(i * rows, (i + 1) * rows) for i in range(TAIL_SUBTILES)]
    d_ff = wg_ref.shape[1]
    bounds = [min(c * FFN_CHUNK, d_ff) for c in range(-(-d_ff // FFN_CHUNK) + 1)]
    chunks = [slice(lo, hi) for lo, hi in zip(bounds[:-1], bounds[1:])]

    mixes = [mix_of_group(i) for i in range(TAIL_SUBTILES)]
    ys = [jnp.dot(mix, wo_ref[...], preferred_element_type=F32) for mix in mixes]
    x1s = [x_ref[r, :] + gate1_ref[...] * _rms(y, g_ref[0:1]) for r, y in zip(groups, ys)]
    hs = [(_rms(x1, g_ref[1:2]) * (1.0 + sc2_ref[...]) + sh2_ref[...]).astype(BF16) for x1 in x1s]
    y2s = None
    for cols in chunks:
        gates = [jnp.dot(h, wg_ref[:, cols], preferred_element_type=F32) for h in hs]
        ups = [jnp.dot(h, wu_ref[:, cols], preferred_element_type=F32) for h in hs]
        acts = [(gate * jax.nn.sigmoid(gate) * up).astype(BF16) for gate, up in zip(gates, ups)]
        parts = [jnp.dot(act, wd_ref[cols, :], preferred_element_type=F32) for act in acts]
        y2s = parts if y2s is None else [y2 + part for y2, part in zip(y2s, parts)]
    for r, x1, y2 in zip(groups, x1s, y2s):
        o_ref[r, :] = x1 + gate2_ref[...] * _rms(y2, g_ref[2:3])


def _token_order(src_ref, buf, dilation, rows, group):
    n_slabs = buf.shape[0]
    n = rows // dilation
    for rho in range(dilation):
        part = src_ref[rho, group * n:(group + 1) * n, :].astype(F32)
        for cb in range(n_slabs):
            buf[cb, pl.ds(group * rows + rho, n, stride=dilation), :] = part[:, cb * LANES:(cb + 1) * LANES]
    return jnp.concatenate([buf[cb, group * rows:(group + 1) * rows, :] for cb in range(n_slabs)], axis=-1)


def _post_even_kernel(x_ref, o1_ref, o4_ref, o16_ref, l1_ref, l4_ref, l16_ref, up_ref, uc_ref, un_ref,
                      expand_ref, band_ref, pw_ref, ps_ref, wo_ref, g_ref, gate1_ref, sh2_ref, sc2_ref, gate2_ref,
                      wg_ref, wu_ref, wd_ref, o_ref, u_buf, o4_buf, o16_buf, l4_buf, l16_buf, *, tm, seq_len):
    rows = tm // TAIL_SUBTILES
    t0 = pl.program_id(1) * tm
    zeros = jnp.zeros((B_PAD - B_HALO, B_WIDTH), BF16)
    u_buf[0:B_PAD - B_HALO] = zeros
    u_buf[B_PAD - B_HALO:B_PAD] = jnp.where(t0 > 0, up_ref[...], jnp.zeros_like(up_ref))
    u_buf[B_PAD:B_PAD + tm] = uc_ref[...]
    u_buf[B_PAD + tm:B_PAD + tm + B_HALO] = jnp.where(t0 + tm < seq_len, un_ref[...], jnp.zeros_like(un_ref))
    u_buf[B_PAD + tm + B_HALO:] = zeros

    def mix_of_group(i):
        r = slice(i * rows, (i + 1) * rows)
        outs = (o1_ref[r, :].astype(F32), _token_order(o4_ref, o4_buf, 4, rows, i),
                _token_order(o16_ref, o16_buf, 16, rows, i))
        l1, l2, l3 = l1_ref[r, :], _token_order(l4_ref, l4_buf, 4, rows, i), _token_order(l16_ref, l16_buf, 16, rows, i)
        mx = jnp.maximum(jnp.maximum(l1, l2), l3)
        es = (jnp.exp(l1 - mx), jnp.exp(l2 - mx), jnp.exp(l3 - mx))
        inv = 1.0 / (es[0] + es[1] + es[2])
        ya = jnp.zeros((rows, A_WIDTH), F32)
        for e, o_br in zip(es, outs):
            wts = jnp.dot((e * inv).astype(BF16), expand_ref[...], preferred_element_type=F32)
            ya = ya + wts * o_br

        t = t0 + i * rows + lax.broadcasted_iota(jnp.int32, (rows, 1), 0)
        yb = []
        for g, w in enumerate(B_WINDOWS):
            cols = slice(g * B_GROUP_DIM, (g + 1) * B_GROUP_DIM)
            total = jnp.dot(band_ref[g], u_buf[i * rows:i * rows + rows + 2 * B_PAD, cols],
                            preferred_element_type=F32)
            cnt = (jnp.clip(t + w // 2, 0, seq_len) - jnp.clip(t - w // 2, 0, seq_len)).astype(F32)
            pooled = total * (1.0 / cnt) - u_buf[B_PAD + i * rows:B_PAD + (i + 1) * rows, cols].astype(F32)
            yg = jnp.dot(pooled.astype(BF16), pw_ref[g], preferred_element_type=F32)
            yb.append(yg * ps_ref[:, cols])
        return jnp.concatenate([ya] + yb, axis=-1).astype(BF16)

    _ffn_tail(x_ref, mix_of_group, wo_ref, g_ref, gate1_ref, sh2_ref, sc2_ref, gate2_ref, wg_ref, wu_ref, wd_ref, o_ref)


def _post_odd_kernel(x_ref, mix_ref, wo_ref, g_ref, gate1_ref, sh2_ref, sc2_ref, gate2_ref,
                     wg_ref, wu_ref, wd_ref, o_ref):
    rows = o_ref.shape[0] // TAIL_SUBTILES
    mix_of_group = lambda i: mix_ref[i * rows:(i + 1) * rows, :]
    _ffn_tail(x_ref, mix_of_group, wo_ref, g_ref, gate1_ref, sh2_ref, sc2_ref, gate2_ref, wg_ref, wu_ref, wd_ref, o_ref)


def _tail_specs(d, d_ff):
    vec = pl.BlockSpec((None, 1, d), lambda i, t: (i, 0, 0))
    return [_const_spec((d, d)), _const_spec((3, d)), vec, vec, vec, vec,
            _const_spec((d, d_ff)), _const_spec((d, d_ff)), _const_spec((d_ff, d))]


def _pool_bands(rows):
    offset = np.arange(rows + 2 * B_PAD)[None, :] - B_PAD - np.arange(rows)[:, None]
    return jnp.asarray(np.stack([(offset >= -(w // 2)) & (offset < w // 2) for w in B_WINDOWS]), BF16)


def _post_even(x, z1, branches, expand, pool_w, pool_scale, w_out, g3, gate1, sh2, sc2, gate2, wg, wu, wd, tm):
    b, l, d = x.shape
    d_ff = wg.shape[1]
    (o1, l1), (o4, l4), (o16, l16) = branches
    halo_blocks = l // B_HALO
    per_tile = tm // B_HALO
    rows = tm // TAIL_SUBTILES
    tok = lambda w: pl.BlockSpec((None, tm, w), lambda i, t: (i, t, 0))
    grouped = lambda dil, w: pl.BlockSpec((None, dil, tm // dil, w), lambda i, t: (i, 0, t, 0))
    u_col = 3
    in_specs = [tok(d), tok(A_WIDTH), grouped(4, A_WIDTH), grouped(16, A_WIDTH),
                tok(LSE_LANES), grouped(4, LSE_LANES), grouped(16, LSE_LANES),
                pl.BlockSpec((None, B_HALO, B_WIDTH), lambda i, t: (i, jnp.maximum(t * per_tile - 1, 0), u_col)),
                pl.BlockSpec((None, tm, B_WIDTH), lambda i, t: (i, t, u_col)),
                pl.BlockSpec((None, B_HALO, B_WIDTH),
                             lambda i, t: (i, jnp.minimum((t + 1) * per_tile, halo_blocks - 1), u_col)),
                _const_spec((LSE_LANES, A_WIDTH)), _const_spec((B_GROUPS, rows, rows + 2 * B_PAD)),
                _const_spec((B_GROUPS, B_GROUP_DIM, B_GROUP_DIM)),
                _const_spec((1, B_WIDTH))] + _tail_specs(d, d_ff)
    return pl.pallas_call(
        functools.partial(_post_even_kernel, tm=tm, seq_len=l),
        grid=(b, l // tm),
        in_specs=in_specs,
        out_specs=tok(d),
        out_shape=jax.ShapeDtypeStruct((b, l, d), F32),
        scratch_shapes=[pltpu.VMEM((tm + 2 * B_PAD, B_WIDTH), BF16),
                        pltpu.VMEM((A_WIDTH // LANES, tm, LANES), F32),
                        pltpu.VMEM((A_WIDTH // LANES, tm, LANES), F32),
                        pltpu.VMEM((LSE_LANES // LANES, tm, LANES), F32),
                        pltpu.VMEM((LSE_LANES // LANES, tm, LANES), F32)],
        compiler_params=_params(2),
        name="post_even",
    )(x, o1.reshape(b, l, A_WIDTH), o4, o16, l1.reshape(b, l, LSE_LANES), l4, l16, z1, z1, z1,
      expand, _pool_bands(rows), pool_w, pool_scale.reshape(1, B_WIDTH), w_out, g3, gate1, sh2, sc2, gate2,
      wg, wu, wd)


def _post_odd(x, mix, w_out, g3, gate1, sh2, sc2, gate2, wg, wu, wd, tm):
    b, l, d = x.shape
    d_ff = wg.shape[1]
    tok = lambda w: pl.BlockSpec((None, tm, w), lambda i, t: (i, t, 0))
    return pl.pallas_call(
        _post_odd_kernel,
        grid=(b, l // tm),
        in_specs=[tok(d), tok(C_WIDTH)] + _tail_specs(d, d_ff),
        out_specs=tok(d),
        out_shape=jax.ShapeDtypeStruct((b, l, d), F32),
        compiler_params=_params(2),
        name="post_odd",
    )(x, mix, w_out, g3, gate1, sh2, sc2, gate2, wg, wu, wd)


def _head_expand_matrix():
    rows = np.arange(LSE_LANES)[:, None]
    cols = np.arange(A_WIDTH)[None, :]
    return jnp.asarray(rows == cols // HEAD_DIM, BF16)


def kernel(x_prompt, x_sample, c_prompt, c_sample, norm_g, ada_w, ada_b, ffn_w_gate, ffn_w_up, ffn_w_down,
           even_w_in, even_w_out, pool_w, pool_scale, t5_table, odd_w_qkv, odd_w_out, odd_rpb):
    depth = norm_g.shape[0]
    d = x_prompt.shape[-1]
    nb_prompt = c_prompt.shape[0]
    mod_all = _ada(jnp.concatenate([c_prompt, c_sample], axis=0), ada_w, ada_b)
    expand = _head_expand_matrix()
    biases = [_dilated_bias(t5_table, dil) for dil in A_DILATIONS]

    layers = []
    for layer in range(depth):
        i = layer // 2
        w = dict(ffn=(ffn_w_gate[layer].astype(BF16), ffn_w_up[layer].astype(BF16), ffn_w_down[layer].astype(BF16)))
        if layer % 2 == 0:
            wq, wk, wv, wu = jnp.split(even_w_in[i], [A_WIDTH, 2 * A_WIDTH, 3 * A_WIDTH], axis=-1)
            w.update(w_in=jnp.concatenate([wk, wv, wq * Q_SCALE, wu], axis=-1).astype(BF16),
                     w_out=even_w_out[i].astype(BF16), pool_w=pool_w[i].astype(BF16), pool_scale=pool_scale[i])
        else:
            wq, wk, wv = jnp.split(odd_w_qkv[i], 3, axis=-1)
            w.update(w_in=jnp.concatenate([wq * Q_SCALE, wk, wv], axis=-1).astype(BF16),
                     w_out=odd_w_out[i].astype(BF16), rpb=_natten_bias(odd_rpb[i]))
        layers.append(w)

    def trunk(x, mod, tm):
        b, l, _ = x.shape
        for layer, w in enumerate(layers):
            sh1, sc1, g1, sh2, sc2, g2 = [m.reshape(b, 1, d) for m in jnp.split(mod[layer], 6, axis=-1)]
            g = norm_g[layer]
            if layer % 2 == 0:
                z1, z4, z16 = _pre_even(x, g[0], sh1, sc1, w["w_in"], tm)
                branches = [_dilated_branch(zd, bias)
                            for zd, bias in zip((z1.reshape(b, 1, l, EVEN_IN), z4, z16), biases)]
                x = _post_even(x, z1, branches, expand, w["pool_w"], w["pool_scale"], w["w_out"],
                               g[1:4], g1, sh2, sc2, g2, *w["ffn"], tm)
            else:
                z = _pre_odd(x, g[0], sh1, sc1, w["w_in"], tm)
                mix = _natten(z, w["rpb"])
                x = _post_odd(x, mix, w["w_out"], g[1:4], g1, sh2, sc2, g2, *w["ffn"], tm)
        return x

    y_prompt = trunk(x_prompt, mod_all[:, :nb_prompt], TOKEN_TILE)
    y_sample = trunk(x_sample, mod_all[:, nb_prompt:], TOKEN_TILE)
    return (y_prompt, y_sample)
```

```python
import functools

import numpy as np
import jax
import jax.numpy as jnp
from jax import lax
from jax.experimental import pallas as pl
from jax.experimental.pallas import tpu as pltpu

D_MODEL = 1024
HEAD_DIM = 64
ATTN_SCALE = HEAD_DIM ** -0.5
LOG2E = float(np.log2(np.e))
LN2 = float(np.log(2.0))
Q_SCALE = ATTN_SCALE * LOG2E
A_HEADS = 8
A_WIDTH = A_HEADS * HEAD_DIM
A_DILATIONS = (1, 4, 16)
A_BLOCK = 128
A_RADIUS = 64
A_CHUNK = 1024
A_LOOKAHEAD = 4
B_GROUPS = 4
B_WINDOWS = (2, 4, 8, 16)
B_WIDTH = 512
B_GROUP_DIM = 128
B_HALO = 16
B_PAD = 64
EVEN_IN = 3 * A_WIDTH + B_WIDTH
C_HEADS = 16
C_WIDTH = C_HEADS * HEAD_DIM
GRID_W = 64
NA_ROWS = 8
NA_COLS = 16
NA_CHUNK_ROWS = 16
NA_QUAD = 4
NA_ROWOFFS = 2 * NA_ROWS - 1
NA_LOOKAHEAD = 2
T5_BUCKETS = 32
T5_MAX_DIST = 1024
NEG = -1e30
EPS = 1e-6
LANES = 128
LSE_LANES = 128
TOKEN_TILE = 512
TAIL_SUBTILES = 2
PRE_SUBTILES = 2
PRE_COL_CHUNK = 1024
FFN_CHUNK = 1536
V7X_VMEM_LIMIT_BYTES = 60 * 1024 * 1024

BF16 = jnp.bfloat16
F32 = jnp.float32


def _const_spec(shape):
    zeros = (0,) * len(shape)
    return pl.BlockSpec(shape, lambda *_: zeros, pipeline_mode=pl.Buffered(1))


def _params(n_axes):
    return pltpu.CompilerParams(dimension_semantics=("arbitrary",) * n_axes,
                                vmem_limit_bytes=V7X_VMEM_LIMIT_BYTES)


def _rms(x, g):
    return x * lax.rsqrt(jnp.mean(x * x, axis=-1, keepdims=True) + EPS) * g


def _ada_kernel(c_ref, w_ref, b_ref, o_ref):
    c = c_ref[...]
    s = (c * jax.nn.sigmoid(c)).astype(BF16)
    o_ref[...] = jnp.dot(s, w_ref[...].astype(BF16), preferred_element_type=F32) + b_ref[...]


def _ada(c_all, ada_w, ada_b):
    depth, d, six_d = ada_w.shape
    nb = c_all.shape[0]
    return pl.pallas_call(
        _ada_kernel,
        grid=(depth, six_d // d),
        in_specs=[pl.BlockSpec((nb, d), lambda l, j: (0, 0)),
                  pl.BlockSpec((None, d, d), lambda l, j: (l, 0, j)),
                  pl.BlockSpec((None, 1, d), lambda l, j: (l, 0, j))],
        out_specs=pl.BlockSpec((None, nb, d), lambda l, j: (l, 0, j)),
        out_shape=jax.ShapeDtypeStruct((depth, nb, six_d), F32),
        compiler_params=_params(2),
        name="ada_mod",
    )(c_all, ada_w, ada_b.reshape(depth, 1, six_d))


def _modulated(x_ref, g_ref, sh_ref, sc_ref):
    return (_rms(x_ref[...], g_ref[0:1]) * (1.0 + sc_ref[...]) + sh_ref[...]).astype(BF16)


def _pre_odd_kernel(x_ref, g_ref, sh_ref, sc_ref, w_ref, o_ref, *, col_chunk):
    hb = _modulated(x_ref, g_ref, sh_ref, sc_ref)
    for j in range(w_ref.shape[1] // col_chunk):
        cols = slice(j * col_chunk, (j + 1) * col_chunk)
        o_ref[:, cols] = jnp.dot(hb, w_ref[:, cols], preferred_element_type=F32).astype(BF16)


def _pre_even_kernel(x_ref, g_ref, sh_ref, sc_ref, w_ref, z1_ref, z4_ref, z16_ref, *zs_bufs, col_chunk, tm):
    per_chunk = col_chunk // LANES
    rows = tm // PRE_SUBTILES
    n_chunks = w_ref.shape[1] // col_chunk
    items = [(sub, j) for sub in range(PRE_SUBTILES) for j in range(n_chunks)]

    def regroup(sub, j):
        zs = zs_bufs[sub * n_chunks + j]
        for cb in range(min(per_chunk, 3 * A_WIDTH // LANES - j * per_chunk)):
            out_cols = slice((j * per_chunk + cb) * LANES, (j * per_chunk + cb + 1) * LANES)
            for dil, ref in ((4, z4_ref), (16, z16_ref)):
                n = rows // dil
                for rho in range(dil):
                    ref[rho, sub * n:(sub + 1) * n, out_cols] = zs[cb, pl.ds(rho, n, stride=dil), :].astype(BF16)

    previous = None
    hbs = {}
    for sub, j in items:
        r = slice(sub * rows, (sub + 1) * rows)
        cols = slice(j * col_chunk, (j + 1) * col_chunk)
        if sub not in hbs:
            hbs[sub] = (_rms(x_ref[r, :], g_ref[0:1]) * (1.0 + sc_ref[...]) + sh_ref[...]).astype(BF16)
        z = jnp.dot(hbs[sub], w_ref[:, cols], preferred_element_type=F32)
        z1_ref[r, cols] = z.astype(BF16)
        for cb in range(per_chunk):
            zs_bufs[sub * n_chunks + j][cb] = z[:, cb * LANES:(cb + 1) * LANES]
        if previous is not None:
            regroup(*previous)
        previous = (sub, j)
    regroup(*previous)


def _cond_specs(d, cond, chunks):
    _, _, layer, first = cond
    vec = lambda j: pl.BlockSpec((None, None, 1, d), lambda i, t: (layer, first + i, 0, j))
    return [pl.BlockSpec((None, 4, d), lambda i, t: (layer, 0, 0))] + [vec(j) for j in chunks]


def _cond_args(cond, n_chunks):
    return (cond[0],) + (cond[1],) * n_chunks


def _pre_specs(d, n, tm, cond):
    return ([pl.BlockSpec((None, tm, d), lambda i, t: (i, t, 0))] + _cond_specs(d, cond, (0, 1))
            + [_const_spec((d, n))])


def _pre_odd(x, cond, w, tm):
    b, l, d = x.shape
    n = w.shape[1]
    return pl.pallas_call(
        functools.partial(_pre_odd_kernel, col_chunk=PRE_COL_CHUNK),
        grid=(b, l // tm),
        in_specs=_pre_specs(d, n, tm, cond),
        out_specs=pl.BlockSpec((None, tm, n), lambda i, t: (i, t, 0)),
        out_shape=jax.ShapeDtypeStruct((b, l, n), BF16),
        compiler_params=_params(2),
        name="pre_odd",
    )(x, *_cond_args(cond, 2), w)


def _pre_even(x, cond, w, tm):
    b, l, d = x.shape
    n = w.shape[1]
    qkv = 3 * A_WIDTH
    return pl.pallas_call(
        functools.partial(_pre_even_kernel, col_chunk=PRE_COL_CHUNK, tm=tm),
        grid=(b, l // tm),
        in_specs=_pre_specs(d, n, tm, cond),
        out_specs=[pl.BlockSpec((None, tm, n), lambda i, t: (i, t, 0)),
                   pl.BlockSpec((None, 4, tm // 4, qkv), lambda i, t: (i, 0, t, 0)),
                   pl.BlockSpec((None, 16, tm // 16, qkv), lambda i, t: (i, 0, t, 0))],
        out_shape=[jax.ShapeDtypeStruct((b, l, n), BF16),
                   jax.ShapeDtypeStruct((b, 4, l // 4, qkv), BF16),
                   jax.ShapeDtypeStruct((b, 16, l // 16, qkv), BF16)],
        scratch_shapes=[pltpu.VMEM((PRE_COL_CHUNK // LANES, tm // PRE_SUBTILES, LANES), F32)
                        for _ in range(PRE_SUBTILES * (n // PRE_COL_CHUNK))],
        compiler_params=_params(2),
        name="pre_even",
    )(x, *_cond_args(cond, 2), w)


def _t5_bucket(rel):
    nb = T5_BUCKETS // 2
    max_exact = nb // 2
    ret = (rel > 0).astype(np.int32) * nb
    n = np.abs(rel)
    large = max_exact + (np.log(np.maximum(n, 1) / max_exact) / np.log(T5_MAX_DIST / max_exact)
                         * (nb - max_exact)).astype(np.int32)
    large = np.minimum(large, nb - 1)
    return (ret + np.where(n < max_exact, n, large)).astype(np.int32)


def _lookup_last(table, idx):
    n = table.shape[-1]
    onehot = jnp.asarray(idx)[None] == jnp.arange(n, dtype=jnp.int32).reshape((n,) + (1,) * idx.ndim)
    expanded = table.astype(F32).reshape(table.shape + (1,) * idx.ndim)
    return jnp.sum(jnp.where(onehot, expanded, 0.0), axis=table.ndim - 1)


def _dilated_bias(t5_table, dilation):
    kw = A_BLOCK + 2 * A_RADIUS
    rel = np.arange(kw)[:, None] - A_RADIUS - np.arange(A_BLOCK)[None, :]
    bias = _lookup_last(jnp.transpose(t5_table), _t5_bucket(rel * dilation))
    band = np.abs(rel) <= A_RADIUS
    variants = []
    for v in range(4):
        row_ok = np.ones((kw, 1), bool)
        if v & 1:
            row_ok[:A_RADIUS] = False
        if v & 2:
            row_ok[kw - A_RADIUS:] = False
        masked = jnp.where((band & row_ok)[None], bias * LOG2E, NEG)
        variants.append(masked.reshape(A_HEADS // 2, 2, kw, A_BLOCK).transpose(0, 2, 1, 3)
                        .reshape(A_HEADS // 2, kw, 2 * A_BLOCK))
    return jnp.stack(variants, axis=0)


def _dilated_kernel(q_ref, kvp_ref, kvc_ref, kvn_ref, bias_ref, o_ref, lse_ref, kv_buf, vt_buf, *, chunk, sub_len, group):
    kw = A_BLOCK + 2 * A_RADIUS
    pair_w = 2 * HEAD_DIM
    n_pairs = A_HEADS // 2
    kv_buf[:, 0:A_RADIUS] = kvp_ref[...]
    kv_buf[:, A_RADIUS:A_RADIUS + chunk] = kvc_ref[...]
    kv_buf[:, A_RADIUS + chunk:] = kvn_ref[...]
    for rl in range(group):
        vt_buf[rl] = kv_buf[rl, :, A_WIDTH:2 * A_WIDTH].T
    chunk_start = pl.program_id(2) * chunk
    first_head = lax.broadcasted_iota(jnp.int32, (A_BLOCK, pair_w), 1) < HEAD_DIM
    items = [(rl, j, jp) for rl in range(group) for j in range(chunk // A_BLOCK) for jp in range(n_pairs)]

    def scores(rl, j, jp):
        row0 = j * A_BLOCK
        pos = chunk_start + row0
        variant = (pos == 0).astype(jnp.int32) + 2 * (pos + A_BLOCK == sub_len).astype(jnp.int32)
        cols = slice(jp * pair_w, (jp + 1) * pair_w)
        qp = q_ref[rl, row0:row0 + A_BLOCK, cols].astype(F32)
        qbd = jnp.concatenate([jnp.where(first_head, qp, 0.0), jnp.where(first_head, 0.0, qp)],
                              axis=0).astype(BF16)
        kp = kv_buf[rl, row0:row0 + kw, cols]
        s = lax.dot_general(kp, qbd, (((1,), (1,)), ((), ())), preferred_element_type=F32)
        return s + bias_ref[variant, jp]

    pending = [scores(*item) for item in items[:A_LOOKAHEAD]]
    lse_rows = []
    for idx, (rl, j, jp) in enumerate(items):
        row0 = j * A_BLOCK
        cols = slice(jp * pair_w, (jp + 1) * pair_w)
        v_t = vt_buf[rl, jp * pair_w:(jp + 1) * pair_w, row0:row0 + kw]
        s = pending.pop(0)
        if idx + A_LOOKAHEAD < len(items):
            pending.append(scores(*items[idx + A_LOOKAHEAD]))
        m = jnp.max(s, axis=0, keepdims=True)
        p = jnp.exp2(s - m)
        den = jnp.sum(p, axis=0, keepdims=True)
        res = jnp.dot(v_t, p.astype(BF16), preferred_element_type=F32)
        res = res * (1.0 / den)
        o_t = jnp.concatenate([res[0:HEAD_DIM, 0:A_BLOCK], res[HEAD_DIM:pair_w, A_BLOCK:2 * A_BLOCK]], axis=0)
        o_ref[rl, row0:row0 + A_BLOCK, cols] = o_t.T.astype(BF16)
        lse = (m + jnp.log2(den)) * LN2
        lse_rows += [lse[:, 0:A_BLOCK], lse[:, A_BLOCK:2 * A_BLOCK]]
        if jp == n_pairs - 1:
            lse_t = jnp.concatenate(lse_rows + [jnp.zeros((LSE_LANES - A_HEADS, A_BLOCK), F32)], axis=0)
            lse_ref[rl, row0:row0 + A_BLOCK, :] = lse_t.T
            lse_rows = []


def _dilated_branch(zd, bias):
    b, dilation, sub_len, _ = zd.shape
    chunk = min(A_CHUNK, sub_len)
    group = min(dilation, A_CHUNK // chunk)
    halo_blocks = sub_len // A_RADIUS
    per_chunk = chunk // A_RADIUS
    kw = A_BLOCK + 2 * A_RADIUS
    kv_w = 2 * A_WIDTH
    return pl.pallas_call(
        functools.partial(_dilated_kernel, chunk=chunk, sub_len=sub_len, group=group),
        grid=(b, dilation // group, sub_len // chunk),
        in_specs=[
            pl.BlockSpec((None, group, chunk, A_WIDTH), lambda i, r, c: (i, r, c, 2)),
            pl.BlockSpec((None, group, A_RADIUS, kv_w),
                         lambda i, r, c: (i, r, jnp.maximum(c * per_chunk - 1, 0), 0)),
            pl.BlockSpec((None, group, chunk, kv_w), lambda i, r, c: (i, r, c, 0)),
            pl.BlockSpec((None, group, A_RADIUS, kv_w),
                         lambda i, r, c: (i, r, jnp.minimum((c + 1) * per_chunk, halo_blocks - 1), 0)),
            _const_spec((4, A_HEADS // 2, kw, 2 * A_BLOCK)),
        ],
        out_specs=[pl.BlockSpec((None, group, chunk, A_WIDTH), lambda i, r, c: (i, r, c, 0)),
                   pl.BlockSpec((None, group, chunk, LSE_LANES), lambda i, r, c: (i, r, c, 0))],
        out_shape=[jax.ShapeDtypeStruct((b, dilation, sub_len, A_WIDTH), BF16),
                   jax.ShapeDtypeStruct((b, dilation, sub_len, LSE_LANES), F32)],
        scratch_shapes=[pltpu.VMEM((group, chunk + 2 * A_RADIUS, kv_w), BF16),
                        pltpu.VMEM((group, A_WIDTH, chunk + 2 * A_RADIUS), BF16)],
        compiler_params=_params(3),
        name=f"dilated_d{dilation}",
    )(zd, zd, zd, zd, bias)


def _natten_bias(rpb):
    c = np.arange(GRID_W)
    cstart = np.clip(c - NA_COLS // 2, 0, GRID_W - NA_COLS)
    colmask = (c[None, :] >= cstart[:, None]) & (c[None, :] < cstart[:, None] + NA_COLS)
    coloff = np.clip(c[None, :] - c[:, None] + NA_COLS - 1, 0, 2 * NA_COLS - 2)
    bias = jnp.where(colmask[None, None], _lookup_last(rpb, coloff.astype(np.int32)) * LOG2E, NEG)
    bias = jnp.transpose(bias, (0, 1, 3, 2))
    bias = bias.reshape(C_HEADS // NA_QUAD, NA_QUAD, NA_ROWOFFS, GRID_W, GRID_W)
    return jnp.transpose(bias, (0, 2, 3, 1, 4)).reshape(C_HEADS // NA_QUAD, NA_ROWOFFS, GRID_W, NA_QUAD * GRID_W)


def _natten_kernel(q_ref, kp_ref, kc_ref, kn_ref, vp_ref, vc_ref, vn_ref, bias_ref, o_ref, k_buf, v_buf,
                   *, grid_rows):
    halo = (NA_ROWS // 2) * GRID_W
    chunk = NA_CHUNK_ROWS * GRID_W
    nkeys = NA_ROWS * GRID_W
    quad_w = NA_QUAD * HEAD_DIM
    pair_w = 2 * HEAD_DIM
    k_buf[0:halo] = kp_ref[...]
    k_buf[halo:halo + chunk] = kc_ref[...]
    k_buf[halo + chunk:] = kn_ref[...]
    v_buf[0:halo] = vp_ref[...]
    v_buf[halo:halo + chunk] = vc_ref[...]
    v_buf[halo + chunk:] = vn_ref[...]
    row_base = pl.program_id(1) * NA_CHUNK_ROWS
    lane_head = lax.broadcasted_iota(jnp.int32, (GRID_W, quad_w), 1) // HEAD_DIM
    first_head = lax.broadcasted_iota(jnp.int32, (GRID_W, pair_w), 1) < HEAD_DIM

    def window(rr):
        r = row_base + rr
        rs = jnp.clip(r - NA_ROWS // 2, 0, grid_rows - NA_ROWS)
        return rs - r + (NA_ROWS - 1), pl.multiple_of((rs - (row_base - NA_ROWS // 2)) * GRID_W, GRID_W)

    windows = [window(rr) for rr in range(NA_CHUNK_ROWS)]
    items = [(rr, g) for rr in range(NA_CHUNK_ROWS) for g in range(C_HEADS // NA_QUAD)]

    def scores(rr, g):
        ro0, k0 = windows[rr]
        cols = slice(g * quad_w, (g + 1) * quad_w)
        qs = q_ref[rr * GRID_W:(rr + 1) * GRID_W, cols].astype(F32)
        qbd = jnp.concatenate([jnp.where(lane_head == hh, qs, 0.0) for hh in range(NA_QUAD)],
                              axis=0).astype(BF16)
        kq = k_buf[pl.ds(k0, nkeys), cols]
        s = lax.dot_general(kq, qbd, (((1,), (1,)), ((), ())), preferred_element_type=F32)
        return jnp.concatenate([s[i * GRID_W:(i + 1) * GRID_W] + bias_ref[g, ro0 + i]
                                for i in range(NA_ROWS)], axis=0)

    pending = [scores(*item) for item in items[:NA_LOOKAHEAD]]
    for idx, (rr, g) in enumerate(items):
        cols = slice(g * quad_w, (g + 1) * quad_w)
        vq = v_buf[pl.ds(windows[rr][1], nkeys), cols]
        s = pending.pop(0)
        if idx + NA_LOOKAHEAD < len(items):
            pending.append(scores(*items[idx + NA_LOOKAHEAD]))
        m = jnp.max(s, axis=0, keepdims=True)
        p = jnp.exp2(s - m)
        den = jnp.sum(p, axis=0, keepdims=True)
        res = lax.dot_general(vq, p.astype(BF16), (((0,), (0,)), ((), ())), preferred_element_type=F32)
        res = res * (1.0 / den)
        slabs = []
        for half in range(NA_QUAD // 2):
            blk = res[half * pair_w:(half + 1) * pair_w, half * pair_w:(half + 1) * pair_w].T
            slabs.append(jnp.where(first_head, blk[0:GRID_W], blk[GRID_W:2 * GRID_W]))
        o_ref[rr * GRID_W:(rr + 1) * GRID_W, cols] = jnp.concatenate(slabs, axis=-1).astype(BF16)


def _natten(z, bias):
    b, l, _ = z.shape
    grid_rows = l // GRID_W
    chunk = NA_CHUNK_ROWS * GRID_W
    halo = (NA_ROWS // 2) * GRID_W
    halo_blocks = l // halo
    per_chunk = chunk // halo

    def cur(col):
        return pl.BlockSpec((None, chunk, C_WIDTH), lambda i, c: (i, c, col))

    def prev(col):
        return pl.BlockSpec((None, halo, C_WIDTH), lambda i, c: (i, jnp.maximum(c * per_chunk - 1, 0), col))

    def nxt(col):
        return pl.BlockSpec((None, halo, C_WIDTH),
                            lambda i, c: (i, jnp.minimum((c + 1) * per_chunk, halo_blocks - 1), col))

    return pl.pallas_call(
        functools.partial(_natten_kernel, grid_rows=grid_rows),
        grid=(b, l // chunk),
        in_specs=[cur(0), prev(1), cur(1), nxt(1), prev(2), cur(2), nxt(2),
                  _const_spec((C_HEADS // NA_QUAD, NA_ROWOFFS, GRID_W, NA_QUAD * GRID_W))],
        out_specs=pl.BlockSpec((None, chunk, C_WIDTH), lambda i, c: (i, c, 0)),
        out_shape=jax.ShapeDtypeStruct((b, l, C_WIDTH), BF16),
        scratch_shapes=[pltpu.VMEM((chunk + 2 * halo, C_WIDTH), BF16),
                        pltpu.VMEM((chunk + 2 * halo, C_WIDTH), BF16)],
        compiler_params=_params(2),
        name="natten",
    )(z, z, z, z, z, z, z, bias)


def _ffn_tail(x_ref, mix_of_group, wo_ref, g_ref, gate1_ref, sh2_ref, sc2_ref, gate2_ref, wg_ref, wu_ref, wd_ref,
              o_ref):
    rows = o_ref.shape[0] // TAIL_SUBTILES
    groups = [slice(i * rows, (i + 1) * rows) for i in range(TAIL_SUBTILES)]
    d_ff = wg_ref.shape[1]
    bounds = [min(c * FFN_CHUNK, d_ff) for c in range(-(-d_ff // FFN_CHUNK) + 1)]
    chunks = [slice(lo, hi) for lo, hi in zip(bounds[:-1], bounds[1:])]

    mixes = [mix_of_group(i) for i in range(TAIL_SUBTILES)]
    ys = [jnp.dot(mix, wo_ref[...], preferred_element_type=F32) for mix in mixes]
    x1s = [x_ref[r, :] + gate1_ref[...] * _rms(y, g_ref[1:2]) for r, y in zip(groups, ys)]
    hs = [(_rms(x1, g_ref[2:3]) * (1.0 + sc2_ref[...]) + sh2_ref[...]).astype(BF16) for x1 in x1s]
    y2s = None
    for cols in chunks:
        gates = [jnp.dot(h, wg_ref[:, cols], preferred_element_type=F32) for h in hs]
        ups = [jnp.dot(h, wu_ref[:, cols], preferred_element_type=F32) for h in hs]
        acts = [(gate * jax.nn.sigmoid(gate) * up).astype(BF16) for gate, up in zip(gates, ups)]
        parts = [jnp.dot(act, wd_ref[cols, :], preferred_element_type=F32) for act in acts]
        y2s = parts if y2s is None else [y2 + part for y2, part in zip(y2s, parts)]
    for r, x1, y2 in zip(groups, x1s, y2s):
        o_ref[r, :] = x1 + gate2_ref[...] * _rms(y2, g_ref[3:4])


def _token_order(src_ref, buf, dilation, rows, group):
    n_slabs = buf.shape[0]
    n = rows // dilation
    for rho in range(dilation):
        part = src_ref[rho, group * n:(group + 1) * n, :].astype(F32)
        for cb in range(n_slabs):
            buf[cb, pl.ds(group * rows + rho, n, stride=dilation), :] = part[:, cb * LANES:(cb + 1) * LANES]
    return jnp.concatenate([buf[cb, group * rows:(group + 1) * rows, :] for cb in range(n_slabs)], axis=-1)


def _post_even_kernel(x_ref, o1_ref, o4_ref, o16_ref, l1_ref, l4_ref, l16_ref, up_ref, uc_ref, un_ref,
                      expand_ref, band_ref, pw_ref, ps_ref, wo_ref, g_ref, gate1_ref, sh2_ref, sc2_ref, gate2_ref,
                      wg_ref, wu_ref, wd_ref, o_ref, u_buf, o4_buf, o16_buf, l4_buf, l16_buf, *, tm, seq_len):
    rows = tm // TAIL_SUBTILES
    t0 = pl.program_id(1) * tm
    zeros = jnp.zeros((B_PAD - B_HALO, B_WIDTH), BF16)
    u_buf[0:B_PAD - B_HALO] = zeros
    u_buf[B_PAD - B_HALO:B_PAD] = jnp.where(t0 > 0, up_ref[...], jnp.zeros_like(up_ref))
    u_buf[B_PAD:B_PAD + tm] = uc_ref[...]
    u_buf[B_PAD + tm:B_PAD + tm + B_HALO] = jnp.where(t0 + tm < seq_len, un_ref[...], jnp.zeros_like(un_ref))
    u_buf[B_PAD + tm + B_HALO:] = zeros

    def mix_of_group(i):
        r = slice(i * rows, (i + 1) * rows)
        outs = (o1_ref[r, :].astype(F32), _token_order(o4_ref, o4_buf, 4, rows, i),
                _token_order(o16_ref, o16_buf, 16, rows, i))
        l1, l2, l3 = l1_ref[r, :], _token_order(l4_ref, l4_buf, 4, rows, i), _token_order(l16_ref, l16_buf, 16, rows, i)
        mx = jnp.maximum(jnp.maximum(l1, l2), l3)
        es = (jnp.exp(l1 - mx), jnp.exp(l2 - mx), jnp.exp(l3 - mx))
        inv = 1.0 / (es[0] + es[1] + es[2])
        ya = jnp.zeros((rows, A_WIDTH), F32)
        for e, o_br in zip(es, outs):
            wts = jnp.dot((e * inv).astype(BF16), expand_ref[...], preferred_element_type=F32)
            ya = ya + wts * o_br

        t = t0 + i * rows + lax.broadcasted_iota(jnp.int32, (rows, 1), 0)
        yb = []
        for g, w in enumerate(B_WINDOWS):
            cols = slice(g * B_GROUP_DIM, (g + 1) * B_GROUP_DIM)
            total = jnp.dot(band_ref[g], u_buf[i * rows:i * rows + rows + 2 * B_PAD, cols],
                            preferred_element_type=F32)
            cnt = (jnp.clip(t + w // 2, 0, seq_len) - jnp.clip(t - w // 2, 0, seq_len)).astype(F32)
            pooled = total * (1.0 / cnt) - u_buf[B_PAD + i * rows:B_PAD + (i + 1) * rows, cols].astype(F32)
            yg = jnp.dot(pooled.astype(BF16), pw_ref[g], preferred_element_type=F32)
            yb.append(yg * ps_ref[:, cols])
        return jnp.concatenate([ya] + yb, axis=-1).astype(BF16)

    _ffn_tail(x_ref, mix_of_group, wo_ref, g_ref, gate1_ref, sh2_ref, sc2_ref, gate2_ref, wg_ref, wu_ref, wd_ref, o_ref)


def _post_odd_kernel(x_ref, mix_ref, wo_ref, g_ref, gate1_ref, sh2_ref, sc2_ref, gate2_ref,
                     wg_ref, wu_ref, wd_ref, o_ref):
    rows = o_ref.shape[0] // TAIL_SUBTILES
    mix_of_group = lambda i: mix_ref[i * rows:(i + 1) * rows, :]
    _ffn_tail(x_ref, mix_of_group, wo_ref, g_ref, gate1_ref, sh2_ref, sc2_ref, gate2_ref, wg_ref, wu_ref, wd_ref, o_ref)


def _tail_specs(d, d_ff, cond):
    return ([_const_spec((d, d))] + _cond_specs(d, cond, (2, 3, 4, 5))
            + [_const_spec((d, d_ff)), _const_spec((d, d_ff)), _const_spec((d_ff, d))])


def _pool_bands(rows):
    offset = np.arange(rows + 2 * B_PAD)[None, :] - B_PAD - np.arange(rows)[:, None]
    return jnp.asarray(np.stack([(offset >= -(w // 2)) & (offset < w // 2) for w in B_WINDOWS]), BF16)


def _post_even(x, z1, branches, expand, pool_w, pool_scale, w_out, cond, wg, wu, wd, tm):
    b, l, d = x.shape
    d_ff = wg.shape[1]
    (o1, l1), (o4, l4), (o16, l16) = branches
    halo_blocks = l // B_HALO
    per_tile = tm // B_HALO
    rows = tm // TAIL_SUBTILES
    tok = lambda w: pl.BlockSpec((None, tm, w), lambda i, t: (i, t, 0))
    grouped = lambda dil, w: pl.BlockSpec((None, dil, tm // dil, w), lambda i, t: (i, 0, t, 0))
    u_col = 3
    in_specs = [tok(d), tok(A_WIDTH), grouped(4, A_WIDTH), grouped(16, A_WIDTH),
                tok(LSE_LANES), grouped(4, LSE_LANES), grouped(16, LSE_LANES),
                pl.BlockSpec((None, B_HALO, B_WIDTH), lambda i, t: (i, jnp.maximum(t * per_tile - 1, 0), u_col)),
                pl.BlockSpec((None, tm, B_WIDTH), lambda i, t: (i, t, u_col)),
                pl.BlockSpec((None, B_HALO, B_WIDTH),
                             lambda i, t: (i, jnp.minimum((t + 1) * per_tile, halo_blocks - 1), u_col)),
                _const_spec((LSE_LANES, A_WIDTH)), _const_spec((B_GROUPS, rows, rows + 2 * B_PAD)),
                _const_spec((B_GROUPS, B_GROUP_DIM, B_GROUP_DIM)),
                _const_spec((1, B_WIDTH))] + _tail_specs(d, d_ff, cond)
    return pl.pallas_call(
        functools.partial(_post_even_kernel, tm=tm, seq_len=l),
        grid=(b, l // tm),
        in_specs=in_specs,
        out_specs=tok(d),
        out_shape=jax.ShapeDtypeStruct((b, l, d), F32),
        scratch_shapes=[pltpu.VMEM((tm + 2 * B_PAD, B_WIDTH), BF16),
                        pltpu.VMEM((A_WIDTH // LANES, tm, LANES), F32),
                        pltpu.VMEM((A_WIDTH // LANES, tm, LANES), F32),
                        pltpu.VMEM((LSE_LANES // LANES, tm, LANES), F32),
                        pltpu.VMEM((LSE_LANES // LANES, tm, LANES), F32)],
        compiler_params=_params(2),
        name="post_even",
    )(x, o1.reshape(b, l, A_WIDTH), o4, o16, l1.reshape(b, l, LSE_LANES), l4, l16, z1, z1, z1,
      expand, _pool_bands(rows), pool_w, pool_scale.reshape(1, B_WIDTH), w_out, *_cond_args(cond, 4), wg, wu, wd)


def _post_odd(x, mix, w_out, cond, wg, wu, wd, tm):
    b, l, d = x.shape
    d_ff = wg.shape[1]
    tok = lambda w: pl.BlockSpec((None, tm, w), lambda i, t: (i, t, 0))
    return pl.pallas_call(
        _post_odd_kernel,
        grid=(b, l // tm),
        in_specs=[tok(d), tok(C_WIDTH)] + _tail_specs(d, d_ff, cond),
        out_specs=tok(d),
        out_shape=jax.ShapeDtypeStruct((b, l, d), F32),
        compiler_params=_params(2),
        name="post_odd",
    )(x, mix, w_out, *_cond_args(cond, 4), wg, wu, wd)


def _head_expand_matrix():
    rows = np.arange(LSE_LANES)[:, None]
    cols = np.arange(A_WIDTH)[None, :]
    return jnp.asarray(rows == cols // HEAD_DIM, BF16)


def kernel(x_prompt, x_sample, c_prompt, c_sample, norm_g, ada_w, ada_b, ffn_w_gate, ffn_w_up, ffn_w_down,
           even_w_in, even_w_out, pool_w, pool_scale, t5_table, odd_w_qkv, odd_w_out, odd_rpb):
    depth = norm_g.shape[0]
    d = x_prompt.shape[-1]
    nb_prompt = c_prompt.shape[0]
    mod_all = _ada(jnp.concatenate([c_prompt, c_sample], axis=0), ada_w, ada_b)
    expand = _head_expand_matrix()
    biases = [_dilated_bias(t5_table, dil) for dil in A_DILATIONS]

    layers = []
    for layer in range(depth):
        i = layer // 2
        w = dict(ffn=(ffn_w_gate[layer].astype(BF16), ffn_w_up[layer].astype(BF16), ffn_w_down[layer].astype(BF16)))
        if layer % 2 == 0:
            wq, wk, wv, wu = jnp.split(even_w_in[i], [A_WIDTH, 2 * A_WIDTH, 3 * A_WIDTH], axis=-1)
            w.update(w_in=jnp.concatenate([wk, wv, wq * Q_SCALE, wu], axis=-1).astype(BF16),
                     w_out=even_w_out[i].astype(BF16), pool_w=pool_w[i].astype(BF16), pool_scale=pool_scale[i])
        else:
            wq, wk, wv = jnp.split(odd_w_qkv[i], 3, axis=-1)
            w.update(w_in=jnp.concatenate([wq * Q_SCALE, wk, wv], axis=-1).astype(BF16),
                     w_out=odd_w_out[i].astype(BF16), rpb=_natten_bias(odd_rpb[i]))
        layers.append(w)

    mod4 = mod_all.reshape(depth, mod_all.shape[1], 1, 6 * d)

    def trunk(x, first_row, tm):
        b, l, _ = x.shape
        for layer, w in enumerate(layers):
            cond = (norm_g, mod4, layer, first_row)
            if layer % 2 == 0:
                z1, z4, z16 = _pre_even(x, cond, w["w_in"], tm)
                branches = [_dilated_branch(zd, bias)
                            for zd, bias in zip((z1.reshape(b, 1, l, EVEN_IN), z4, z16), biases)]
                x = _post_even(x, z1, branches, expand, w["pool_w"], w["pool_scale"], w["w_out"], cond,
                               *w["ffn"], tm)
            else:
                z = _pre_odd(x, cond, w["w_in"], tm)
                mix = _natten(z, w["rpb"])
                x = _post_odd(x, mix, w["w_out"], cond, *w["ffn"], tm)
        return x

    y_prompt = trunk(x_prompt, 0, TOKEN_TILE)
    y_sample = trunk(x_sample, nb_prompt, TOKEN_TILE)
    return (y_prompt, y_sample)
```

```python
import functools

import numpy as np
import jax
import jax.numpy as jnp
from jax import lax
from jax.experimental import pallas as pl
from jax.experimental.pallas import tpu as pltpu

D_MODEL = 1024
HEAD_DIM = 64
ATTN_SCALE = HEAD_DIM ** -0.5
LOG2E = float(np.log2(np.e))
LN2 = float(np.log(2.0))
Q_SCALE = ATTN_SCALE * LOG2E
A_HEADS = 8
A_WIDTH = A_HEADS * HEAD_DIM
A_DILATIONS = (1, 4, 16)
A_BLOCK = 128
A_RADIUS = 64
A_CHUNK = 1024
A_LOOKAHEAD = 4
B_GROUPS = 4
B_WINDOWS = (2, 4, 8, 16)
B_WIDTH = 512
B_GROUP_DIM = 128
B_HALO = 16
B_PAD = 64
EVEN_IN = 3 * A_WIDTH + B_WIDTH
C_HEADS = 16
C_WIDTH = C_HEADS * HEAD_DIM
GRID_W = 64
NA_ROWS = 8
NA_COLS = 16
NA_CHUNK_ROWS = 16
NA_QUAD = 4
NA_ROWOFFS = 2 * NA_ROWS - 1
NA_LOOKAHEAD = 2
T5_BUCKETS = 32
T5_MAX_DIST = 1024
NEG = -1e30
EPS = 1e-6
LANES = 128
LSE_LANES = 128
TOKEN_TILE = 512
TAIL_SUBTILES = 2
PRE_SUBTILES = 2
PRE_COL_CHUNK = 1024
FFN_CHUNK = 1536
V7X_VMEM_LIMIT_BYTES = 60 * 1024 * 1024

BF16 = jnp.bfloat16
F32 = jnp.float32


def _const_spec(shape):
    zeros = (0,) * len(shape)
    return pl.BlockSpec(shape, lambda *_: zeros, pipeline_mode=pl.Buffered(1))


def _params(n_axes):
    return pltpu.CompilerParams(dimension_semantics=("arbitrary",) * n_axes,
                                vmem_limit_bytes=V7X_VMEM_LIMIT_BYTES)


def _rms(x, g):
    return x * lax.rsqrt(jnp.mean(x * x, axis=-1, keepdims=True) + EPS) * g


def _ada_kernel(c_ref, w_ref, b_ref, o_ref):
    c = c_ref[...]
    s = (c * jax.nn.sigmoid(c)).astype(BF16)
    o_ref[...] = jnp.dot(s, w_ref[...].astype(BF16), preferred_element_type=F32) + b_ref[...]


def _ada(c_all, ada_w, ada_b):
    depth, d, six_d = ada_w.shape
    nb = c_all.shape[0]
    return pl.pallas_call(
        _ada_kernel,
        grid=(depth, six_d // d),
        in_specs=[pl.BlockSpec((nb, d), lambda l, j: (0, 0)),
                  pl.BlockSpec((None, d, d), lambda l, j: (l, 0, j)),
                  pl.BlockSpec((None, 1, d), lambda l, j: (l, 0, j))],
        out_specs=pl.BlockSpec((None, nb, d), lambda l, j: (l, 0, j)),
        out_shape=jax.ShapeDtypeStruct((depth, nb, six_d), F32),
        compiler_params=_params(2),
        name="ada_mod",
    )(c_all, ada_w, ada_b.reshape(depth, 1, six_d))


def _modulated(x_ref, g_ref, sh_ref, sc_ref):
    return (_rms(x_ref[...], g_ref[0:1]) * (1.0 + sc_ref[...]) + sh_ref[...]).astype(BF16)


def _pre_odd_kernel(x_ref, g_ref, sh_ref, sc_ref, w_ref, o_ref, *, col_chunk):
    hb = _modulated(x_ref, g_ref, sh_ref, sc_ref)
    for j in range(w_ref.shape[1] // col_chunk):
        cols = slice(j * col_chunk, (j + 1) * col_chunk)
        o_ref[:, cols] = jnp.dot(hb, w_ref[:, cols], preferred_element_type=F32).astype(BF16)


def _pre_even_kernel(x_ref, g_ref, sh_ref, sc_ref, w_ref, perm_ref, z1_ref, z4_ref, z16_ref, *, col_chunk, tm):
    rows = tm // PRE_SUBTILES
    attn_w = 3 * A_WIDTH
    for sub in range(PRE_SUBTILES):
        r = slice(sub * rows, (sub + 1) * rows)
        hb = (_rms(x_ref[r, :], g_ref[0:1]) * (1.0 + sc_ref[...]) + sh_ref[...]).astype(BF16)
        for j in range(w_ref.shape[1] // col_chunk):
            cols = slice(j * col_chunk, (j + 1) * col_chunk)
            zb = jnp.dot(hb, w_ref[:, cols], preferred_element_type=F32).astype(BF16)
            z1_ref[r, cols] = zb
            width = min(col_chunk, attn_w - j * col_chunk)
            if width <= 0:
                continue
            out_cols = slice(j * col_chunk, j * col_chunk + width)
            for k, (dil, ref) in enumerate(((4, z4_ref), (16, z16_ref))):
                n = rows // dil
                zp = jnp.dot(perm_ref[k], zb[:, :width], preferred_element_type=F32).astype(BF16)
                ref[:, sub * n:(sub + 1) * n, out_cols] = zp.reshape(dil, n, width)


def _cond_specs(d, cond, chunks):
    _, _, layer, first = cond
    vec = lambda j: pl.BlockSpec((None, None, 1, d), lambda i, t: (layer, first + i, 0, j))
    return [pl.BlockSpec((None, 4, d), lambda i, t: (layer, 0, 0))] + [vec(j) for j in chunks]


def _cond_args(cond, n_chunks):
    return (cond[0],) + (cond[1],) * n_chunks


def _pre_specs(d, n, tm, cond):
    return ([pl.BlockSpec((None, tm, d), lambda i, t: (i, t, 0))] + _cond_specs(d, cond, (0, 1))
            + [_const_spec((d, n))])


def _pre_odd(x, cond, w, tm):
    b, l, d = x.shape
    n = w.shape[1]
    return pl.pallas_call(
        functools.partial(_pre_odd_kernel, col_chunk=PRE_COL_CHUNK),
        grid=(b, l // tm),
        in_specs=_pre_specs(d, n, tm, cond),
        out_specs=pl.BlockSpec((None, tm, n), lambda i, t: (i, t, 0)),
        out_shape=jax.ShapeDtypeStruct((b, l, n), BF16),
        compiler_params=_params(2),
        name="pre_odd",
    )(x, *_cond_args(cond, 2), w)


def _pre_even(x, cond, w, tm):
    b, l, d = x.shape
    n = w.shape[1]
    qkv = 3 * A_WIDTH
    rows = tm // PRE_SUBTILES
    return pl.pallas_call(
        functools.partial(_pre_even_kernel, col_chunk=PRE_COL_CHUNK, tm=tm),
        grid=(b, l // tm),
        in_specs=_pre_specs(d, n, tm, cond) + [_const_spec((2, rows, rows))],
        out_specs=[pl.BlockSpec((None, tm, n), lambda i, t: (i, t, 0)),
                   pl.BlockSpec((None, 4, tm // 4, qkv), lambda i, t: (i, 0, t, 0)),
                   pl.BlockSpec((None, 16, tm // 16, qkv), lambda i, t: (i, 0, t, 0))],
        out_shape=[jax.ShapeDtypeStruct((b, l, n), BF16),
                   jax.ShapeDtypeStruct((b, 4, l // 4, qkv), BF16),
                   jax.ShapeDtypeStruct((b, 16, l // 16, qkv), BF16)],
        compiler_params=_params(2),
        name="pre_even",
    )(x, *_cond_args(cond, 2), w, _residue_permutations(rows, (4, 16)))


def _residue_permutations(rows, dilations):
    out = np.arange(rows)[:, None]
    src = np.arange(rows)[None, :]
    return jnp.asarray(np.stack([src == dil * (out % (rows // dil)) + out // (rows // dil) for dil in dilations]),
                       BF16)


def _t5_bucket(rel):
    nb = T5_BUCKETS // 2
    max_exact = nb // 2
    ret = (rel > 0).astype(np.int32) * nb
    n = np.abs(rel)
    large = max_exact + (np.log(np.maximum(n, 1) / max_exact) / np.log(T5_MAX_DIST / max_exact)
                         * (nb - max_exact)).astype(np.int32)
    large = np.minimum(large, nb - 1)
    return (ret + np.where(n < max_exact, n, large)).astype(np.int32)


def _lookup_last(table, idx):
    n = table.shape[-1]
    onehot = jnp.asarray(idx)[None] == jnp.arange(n, dtype=jnp.int32).reshape((n,) + (1,) * idx.ndim)
    expanded = table.astype(F32).reshape(table.shape + (1,) * idx.ndim)
    return jnp.sum(jnp.where(onehot, expanded, 0.0), axis=table.ndim - 1)


def _dilated_bias(t5_table, dilation):
    kw = A_BLOCK + 2 * A_RADIUS
    rel = np.arange(kw)[:, None] - A_RADIUS - np.arange(A_BLOCK)[None, :]
    bias = _lookup_last(jnp.transpose(t5_table), _t5_bucket(rel * dilation))
    band = np.abs(rel) <= A_RADIUS
    variants = []
    for v in range(4):
        row_ok = np.ones((kw, 1), bool)
        if v & 1:
            row_ok[:A_RADIUS] = False
        if v & 2:
            row_ok[kw - A_RADIUS:] = False
        masked = jnp.where((band & row_ok)[None], bias * LOG2E, NEG)
        variants.append(masked.reshape(A_HEADS // 2, 2, kw, A_BLOCK).transpose(0, 2, 1, 3)
                        .reshape(A_HEADS // 2, kw, 2 * A_BLOCK))
    return jnp.stack(variants, axis=0)


def _dilated_kernel(q_ref, kvp_ref, kvc_ref, kvn_ref, bias_ref, o_ref, lse_ref, kv_buf, vt_buf, *, chunk, sub_len, group):
    kw = A_BLOCK + 2 * A_RADIUS
    pair_w = 2 * HEAD_DIM
    n_pairs = A_HEADS // 2
    kv_buf[:, 0:A_RADIUS] = kvp_ref[...]
    kv_buf[:, A_RADIUS:A_RADIUS + chunk] = kvc_ref[...]
    kv_buf[:, A_RADIUS + chunk:] = kvn_ref[...]
    for rl in range(group):
        vt_buf[rl] = kv_buf[rl, :, A_WIDTH:2 * A_WIDTH].T
    chunk_start = pl.program_id(2) * chunk
    first_head = lax.broadcasted_iota(jnp.int32, (A_BLOCK, pair_w), 1) < HEAD_DIM
    items = [(rl, j, jp) for rl in range(group) for j in range(chunk // A_BLOCK) for jp in range(n_pairs)]

    def scores(rl, j, jp):
        row0 = j * A_BLOCK
        pos = chunk_start + row0
        variant = (pos == 0).astype(jnp.int32) + 2 * (pos + A_BLOCK == sub_len).astype(jnp.int32)
        cols = slice(jp * pair_w, (jp + 1) * pair_w)
        qp = q_ref[rl, row0:row0 + A_BLOCK, cols].astype(F32)
        qbd = jnp.concatenate([jnp.where(first_head, qp, 0.0), jnp.where(first_head, 0.0, qp)],
                              axis=0).astype(BF16)
        kp = kv_buf[rl, row0:row0 + kw, cols]
        s = lax.dot_general(kp, qbd, (((1,), (1,)), ((), ())), preferred_element_type=F32)
        return s + bias_ref[variant, jp]

    pending = [scores(*item) for item in items[:A_LOOKAHEAD]]
    lse_rows = []
    for idx, (rl, j, jp) in enumerate(items):
        row0 = j * A_BLOCK
        cols = slice(jp * pair_w, (jp + 1) * pair_w)
        v_t = vt_buf[rl, jp * pair_w:(jp + 1) * pair_w, row0:row0 + kw]
        s = pending.pop(0)
        if idx + A_LOOKAHEAD < len(items):
            pending.append(scores(*items[idx + A_LOOKAHEAD]))
        m = jnp.max(s, axis=0, keepdims=True)
        p = jnp.exp2(s - m)
        den = jnp.sum(p, axis=0, keepdims=True)
        res = jnp.dot(v_t, p.astype(BF16), preferred_element_type=F32)
        res = res * (1.0 / den)
        o_t = jnp.concatenate([res[0:HEAD_DIM, 0:A_BLOCK], res[HEAD_DIM:pair_w, A_BLOCK:2 * A_BLOCK]], axis=0)
        o_ref[rl, row0:row0 + A_BLOCK, cols] = o_t.T.astype(BF16)
        lse = (m + jnp.log2(den)) * LN2
        lse_rows += [lse[:, 0:A_BLOCK], lse[:, A_BLOCK:2 * A_BLOCK]]
        if jp == n_pairs - 1:
            lse_t = jnp.concatenate(lse_rows + [jnp.zeros((LSE_LANES - A_HEADS, A_BLOCK), F32)], axis=0)
            lse_ref[rl, row0:row0 + A_BLOCK, :] = lse_t.T
            lse_rows = []


def _dilated_branch(zd, bias):
    b, dilation, sub_len, _ = zd.shape
    chunk = min(A_CHUNK, sub_len)
    group = min(dilation, A_CHUNK // chunk)
    halo_blocks = sub_len // A_RADIUS
    per_chunk = chunk // A_RADIUS
    kw = A_BLOCK + 2 * A_RADIUS
    kv_w = 2 * A_WIDTH
    return pl.pallas_call(
        functools.partial(_dilated_kernel, chunk=chunk, sub_len=sub_len, group=group),
        grid=(b, dilation // group, sub_len // chunk),
        in_specs=[
            pl.BlockSpec((None, group, chunk, A_WIDTH), lambda i, r, c: (i, r, c, 2)),
            pl.BlockSpec((None, group, A_RADIUS, kv_w),
                         lambda i, r, c: (i, r, jnp.maximum(c * per_chunk - 1, 0), 0)),
            pl.BlockSpec((None, group, chunk, kv_w), lambda i, r, c: (i, r, c, 0)),
            pl.BlockSpec((None, group, A_RADIUS, kv_w),
                         lambda i, r, c: (i, r, jnp.minimum((c + 1) * per_chunk, halo_blocks - 1), 0)),
            _const_spec((4, A_HEADS // 2, kw, 2 * A_BLOCK)),
        ],
        out_specs=[pl.BlockSpec((None, group, chunk, A_WIDTH), lambda i, r, c: (i, r, c, 0)),
                   pl.BlockSpec((None, group, chunk, LSE_LANES), lambda i, r, c: (i, r, c, 0))],
        out_shape=[jax.ShapeDtypeStruct((b, dilation, sub_len, A_WIDTH), BF16),
                   jax.ShapeDtypeStruct((b, dilation, sub_len, LSE_LANES), F32)],
        scratch_shapes=[pltpu.VMEM((group, chunk + 2 * A_RADIUS, kv_w), BF16),
                        pltpu.VMEM((group, A_WIDTH, chunk + 2 * A_RADIUS), BF16)],
        compiler_params=_params(3),
        name=f"dilated_d{dilation}",
    )(zd, zd, zd, zd, bias)


def _natten_bias(rpb):
    c = np.arange(GRID_W)
    cstart = np.clip(c - NA_COLS // 2, 0, GRID_W - NA_COLS)
    colmask = (c[None, :] >= cstart[:, None]) & (c[None, :] < cstart[:, None] + NA_COLS)
    coloff = np.clip(c[None, :] - c[:, None] + NA_COLS - 1, 0, 2 * NA_COLS - 2)
    bias = jnp.where(colmask[None, None], _lookup_last(rpb, coloff.astype(np.int32)) * LOG2E, NEG)
    bias = jnp.transpose(bias, (0, 1, 3, 2))
    bias = bias.reshape(C_HEADS // NA_QUAD, NA_QUAD, NA_ROWOFFS, GRID_W, GRID_W)
    return jnp.transpose(bias, (0, 2, 3, 1, 4)).reshape(C_HEADS // NA_QUAD, NA_ROWOFFS, GRID_W, NA_QUAD * GRID_W)


def _natten_kernel(q_ref, kp_ref, kc_ref, kn_ref, vp_ref, vc_ref, vn_ref, bias_ref, o_ref, k_buf, v_buf,
                   *, grid_rows):
    halo = (NA_ROWS // 2) * GRID_W
    chunk = NA_CHUNK_ROWS * GRID_W
    nkeys = NA_ROWS * GRID_W
    quad_w = NA_QUAD * HEAD_DIM
    pair_w = 2 * HEAD_DIM
    k_buf[0:halo] = kp_ref[...]
    k_buf[halo:halo + chunk] = kc_ref[...]
    k_buf[halo + chunk:] = kn_ref[...]
    v_buf[0:halo] = vp_ref[...]
    v_buf[halo:halo + chunk] = vc_ref[...]
    v_buf[halo + chunk:] = vn_ref[...]
    row_base = pl.program_id(1) * NA_CHUNK_ROWS
    lane_head = lax.broadcasted_iota(jnp.int32, (GRID_W, quad_w), 1) // HEAD_DIM
    first_head = lax.broadcasted_iota(jnp.int32, (GRID_W, pair_w), 1) < HEAD_DIM

    def window(rr):
        r = row_base + rr
        rs = jnp.clip(r - NA_ROWS // 2, 0, grid_rows - NA_ROWS)
        return rs - r + (NA_ROWS - 1), pl.multiple_of((rs - (row_base - NA_ROWS // 2)) * GRID_W, GRID_W)

    windows = [window(rr) for rr in range(NA_CHUNK_ROWS)]
    items = [(rr, g) for rr in range(NA_CHUNK_ROWS) for g in range(C_HEADS // NA_QUAD)]

    def scores(rr, g):
        ro0, k0 = windows[rr]
        cols = slice(g * quad_w, (g + 1) * quad_w)
        qs = q_ref[rr * GRID_W:(rr + 1) * GRID_W, cols].astype(F32)
        qbd = jnp.concatenate([jnp.where(lane_head == hh, qs, 0.0) for hh in range(NA_QUAD)],
                              axis=0).astype(BF16)
        kq = k_buf[pl.ds(k0, nkeys), cols]
        s = lax.dot_general(kq, qbd, (((1,), (1,)), ((), ())), preferred_element_type=F32)
        return jnp.concatenate([s[i * GRID_W:(i + 1) * GRID_W] + bias_ref[g, ro0 + i]
                                for i in range(NA_ROWS)], axis=0)

    pending = [scores(*item) for item in items[:NA_LOOKAHEAD]]
    for idx, (rr, g) in enumerate(items):
        cols = slice(g * quad_w, (g + 1) * quad_w)
        vq = v_buf[pl.ds(windows[rr][1], nkeys), cols]
        s = pending.pop(0)
        if idx + NA_LOOKAHEAD < len(items):
            pending.append(scores(*items[idx + NA_LOOKAHEAD]))
        m = jnp.max(s, axis=0, keepdims=True)
        p = jnp.exp2(s - m)
        den = jnp.sum(p, axis=0, keepdims=True)
        res = lax.dot_general(vq, p.astype(BF16), (((0,), (0,)), ((), ())), preferred_element_type=F32)
        res = res * (1.0 / den)
        slabs = []
        for half in range(NA_QUAD // 2):
            blk = res[half * pair_w:(half + 1) * pair_w, half * pair_w:(half + 1) * pair_w].T
            slabs.append(jnp.where(first_head, blk[0:GRID_W], blk[GRID_W:2 * GRID_W]))
        o_ref[rr * GRID_W:(rr + 1) * GRID_W, cols] = jnp.concatenate(slabs, axis=-1).astype(BF16)


def _natten(z, bias):
    b, l, _ = z.shape
    grid_rows = l // GRID_W
    chunk = NA_CHUNK_ROWS * GRID_W
    halo = (NA_ROWS // 2) * GRID_W
    halo_blocks = l // halo
    per_chunk = chunk // halo

    def cur(col):
        return pl.BlockSpec((None, chunk, C_WIDTH), lambda i, c: (i, c, col))

    def prev(col):
        return pl.BlockSpec((None, halo, C_WIDTH), lambda i, c: (i, jnp.maximum(c * per_chunk - 1, 0), col))

    def nxt(col):
        return pl.BlockSpec((None, halo, C_WIDTH),
                            lambda i, c: (i, jnp.minimum((c + 1) * per_chunk, halo_blocks - 1), col))

    return pl.pallas_call(
        functools.partial(_natten_kernel, grid_rows=grid_rows),
        grid=(b, l // chunk),
        in_specs=[cur(0), prev(1), cur(1), nxt(1), prev(2), cur(2), nxt(2),
                  _const_spec((C_HEADS // NA_QUAD, NA_ROWOFFS, GRID_W, NA_QUAD * GRID_W))],
        out_specs=pl.BlockSpec((None, chunk, C_WIDTH), lambda i, c: (i, c, 0)),
        out_shape=jax.ShapeDtypeStruct((b, l, C_WIDTH), BF16),
        scratch_shapes=[pltpu.VMEM((chunk + 2 * halo, C_WIDTH), BF16),
                        pltpu.VMEM((chunk + 2 * halo, C_WIDTH), BF16)],
        compiler_params=_params(2),
        name="natten",
    )(z, z, z, z, z, z, z, bias)


def _ffn_tail(x_ref, mix_of_group, wo_ref, g_ref, gate1_ref, sh2_ref, sc2_ref, gate2_ref, wg_ref, wu_ref, wd_ref,
              o_ref):
    rows = o_ref.shape[0] // TAIL_SUBTILES
    groups = [slice(i * rows, (i + 1) * rows) for i in range(TAIL_SUBTILES)]
    d_ff = wg_ref.shape[1]
    bounds = [min(c * FFN_CHUNK, d_ff) for c in range(-(-d_ff // FFN_CHUNK) + 1)]
    chunks = [slice(lo, hi) for lo, hi in zip(bounds[:-1], bounds[1:])]

    mixes = [mix_of_group(i) for i in range(TAIL_SUBTILES)]
    ys = [jnp.dot(mix, wo_ref[...], preferred_element_type=F32) for mix in mixes]
    x1s = [x_ref[r, :] + gate1_ref[...] * _rms(y, g_ref[1:2]) for r, y in zip(groups, ys)]
    hs = [(_rms(x1, g_ref[2:3]) * (1.0 + sc2_ref[...]) + sh2_ref[...]).astype(BF16) for x1 in x1s]
    y2s = None
    for cols in chunks:
        gates = [jnp.dot(h, wg_ref[:, cols], preferred_element_type=F32) for h in hs]
        ups = [jnp.dot(h, wu_ref[:, cols], preferred_element_type=F32) for h in hs]
        acts = [(gate * jax.nn.sigmoid(gate) * up).astype(BF16) for gate, up in zip(gates, ups)]
        parts = [jnp.dot(act, wd_ref[cols, :], preferred_element_type=F32) for act in acts]
        y2s = parts if y2s is None else [y2 + part for y2, part in zip(y2s, parts)]
    for r, x1, y2 in zip(groups, x1s, y2s):
        o_ref[r, :] = x1 + gate2_ref[...] * _rms(y2, g_ref[3:4])


def _token_order(src_ref, buf, dilation, rows, group):
    n_slabs = buf.shape[0]
    n = rows // dilation
    for rho in range(dilation):
        part = src_ref[rho, group * n:(group + 1) * n, :].astype(F32)
        for cb in range(n_slabs):
            buf[cb, pl.ds(group * rows + rho, n, stride=dilation), :] = part[:, cb * LANES:(cb + 1) * LANES]
    return jnp.concatenate([buf[cb, group * rows:(group + 1) * rows, :] for cb in range(n_slabs)], axis=-1)


def _post_even_kernel(x_ref, o1_ref, o4_ref, o16_ref, l1_ref, l4_ref, l16_ref, up_ref, uc_ref, un_ref,
                      expand_ref, unperm_ref, band_ref, pw_ref, ps_ref, wo_ref, g_ref, gate1_ref, sh2_ref, sc2_ref,
                      gate2_ref, wg_ref, wu_ref, wd_ref, o_ref, u_buf, l4_buf, l16_buf, *, tm, seq_len):
    rows = tm // TAIL_SUBTILES
    t0 = pl.program_id(1) * tm
    zeros = jnp.zeros((B_PAD - B_HALO, B_WIDTH), BF16)
    u_buf[0:B_PAD - B_HALO] = zeros
    u_buf[B_PAD - B_HALO:B_PAD] = jnp.where(t0 > 0, up_ref[...], jnp.zeros_like(up_ref))
    u_buf[B_PAD:B_PAD + tm] = uc_ref[...]
    u_buf[B_PAD + tm:B_PAD + tm + B_HALO] = jnp.where(t0 + tm < seq_len, un_ref[...], jnp.zeros_like(un_ref))
    u_buf[B_PAD + tm + B_HALO:] = zeros

    def mix_of_group(i):
        r = slice(i * rows, (i + 1) * rows)
        outs = [o1_ref[r, :].astype(F32)]
        for k, (dil, ref) in enumerate(((4, o4_ref), (16, o16_ref))):
            n = rows // dil
            grouped = ref[:, i * n:(i + 1) * n, :].reshape(rows, A_WIDTH)
            outs.append(jnp.dot(unperm_ref[k], grouped, preferred_element_type=F32))
        l1, l2, l3 = l1_ref[r, :], _token_order(l4_ref, l4_buf, 4, rows, i), _token_order(l16_ref, l16_buf, 16, rows, i)
        mx = jnp.maximum(jnp.maximum(l1, l2), l3)
        es = (jnp.exp(l1 - mx), jnp.exp(l2 - mx), jnp.exp(l3 - mx))
        inv = 1.0 / (es[0] + es[1] + es[2])
        ya = jnp.zeros((rows, A_WIDTH), F32)
        for e, o_br in zip(es, outs):
            wts = jnp.dot((e * inv).astype(BF16), expand_ref[...], preferred_element_type=F32)
            ya = ya + wts * o_br

        t = t0 + i * rows + lax.broadcasted_iota(jnp.int32, (rows, 1), 0)
        yb = []
        for g, w in enumerate(B_WINDOWS):
            cols = slice(g * B_GROUP_DIM, (g + 1) * B_GROUP_DIM)
            total = jnp.dot(band_ref[g], u_buf[i * rows:i * rows + rows + 2 * B_PAD, cols],
                            preferred_element_type=F32)
            cnt = (jnp.clip(t + w // 2, 0, seq_len) - jnp.clip(t - w // 2, 0, seq_len)).astype(F32)
            pooled = total * (1.0 / cnt) - u_buf[B_PAD + i * rows:B_PAD + (i + 1) * rows, cols].astype(F32)
            yg = jnp.dot(pooled.astype(BF16), pw_ref[g], preferred_element_type=F32)
            yb.append(yg * ps_ref[:, cols])
        return jnp.concatenate([ya] + yb, axis=-1).astype(BF16)

    _ffn_tail(x_ref, mix_of_group, wo_ref, g_ref, gate1_ref, sh2_ref, sc2_ref, gate2_ref, wg_ref, wu_ref, wd_ref, o_ref)


def _post_odd_kernel(x_ref, mix_ref, wo_ref, g_ref, gate1_ref, sh2_ref, sc2_ref, gate2_ref,
                     wg_ref, wu_ref, wd_ref, o_ref):
    rows = o_ref.shape[0] // TAIL_SUBTILES
    mix_of_group = lambda i: mix_ref[i * rows:(i + 1) * rows, :]
    _ffn_tail(x_ref, mix_of_group, wo_ref, g_ref, gate1_ref, sh2_ref, sc2_ref, gate2_ref, wg_ref, wu_ref, wd_ref, o_ref)


def _tail_specs(d, d_ff, cond):
    return ([_const_spec((d, d))] + _cond_specs(d, cond, (2, 3, 4, 5))
            + [_const_spec((d, d_ff)), _const_spec((d, d_ff)), _const_spec((d_ff, d))])


def _pool_bands(rows):
    offset = np.arange(rows + 2 * B_PAD)[None, :] - B_PAD - np.arange(rows)[:, None]
    return jnp.asarray(np.stack([(offset >= -(w // 2)) & (offset < w // 2) for w in B_WINDOWS]), BF16)


def _post_even(x, z1, branches, expand, pool_w, pool_scale, w_out, cond, wg, wu, wd, tm):
    b, l, d = x.shape
    d_ff = wg.shape[1]
    (o1, l1), (o4, l4), (o16, l16) = branches
    halo_blocks = l // B_HALO
    per_tile = tm // B_HALO
    rows = tm // TAIL_SUBTILES
    tok = lambda w: pl.BlockSpec((None, tm, w), lambda i, t: (i, t, 0))
    grouped = lambda dil, w: pl.BlockSpec((None, dil, tm // dil, w), lambda i, t: (i, 0, t, 0))
    u_col = 3
    in_specs = [tok(d), tok(A_WIDTH), grouped(4, A_WIDTH), grouped(16, A_WIDTH),
                tok(LSE_LANES), grouped(4, LSE_LANES), grouped(16, LSE_LANES),
                pl.BlockSpec((None, B_HALO, B_WIDTH), lambda i, t: (i, jnp.maximum(t * per_tile - 1, 0), u_col)),
                pl.BlockSpec((None, tm, B_WIDTH), lambda i, t: (i, t, u_col)),
                pl.BlockSpec((None, B_HALO, B_WIDTH),
                             lambda i, t: (i, jnp.minimum((t + 1) * per_tile, halo_blocks - 1), u_col)),
                _const_spec((LSE_LANES, A_WIDTH)), _const_spec((2, rows, rows)),
                _const_spec((B_GROUPS, rows, rows + 2 * B_PAD)),
                _const_spec((B_GROUPS, B_GROUP_DIM, B_GROUP_DIM)),
                _const_spec((1, B_WIDTH))] + _tail_specs(d, d_ff, cond)
    return pl.pallas_call(
        functools.partial(_post_even_kernel, tm=tm, seq_len=l),
        grid=(b, l // tm),
        in_specs=in_specs,
        out_specs=tok(d),
        out_shape=jax.ShapeDtypeStruct((b, l, d), F32),
        scratch_shapes=[pltpu.VMEM((tm + 2 * B_PAD, B_WIDTH), BF16),
                        pltpu.VMEM((LSE_LANES // LANES, tm, LANES), F32),
                        pltpu.VMEM((LSE_LANES // LANES, tm, LANES), F32)],
        compiler_params=_params(2),
        name="post_even",
    )(x, o1.reshape(b, l, A_WIDTH), o4, o16, l1.reshape(b, l, LSE_LANES), l4, l16, z1, z1, z1,
      expand, jnp.swapaxes(_residue_permutations(rows, (4, 16)), 1, 2), _pool_bands(rows), pool_w,
      pool_scale.reshape(1, B_WIDTH), w_out, *_cond_args(cond, 4), wg, wu, wd)


def _post_odd(x, mix, w_out, cond, wg, wu, wd, tm):
    b, l, d = x.shape
    d_ff = wg.shape[1]
    tok = lambda w: pl.BlockSpec((None, tm, w), lambda i, t: (i, t, 0))
    return pl.pallas_call(
        _post_odd_kernel,
        grid=(b, l // tm),
        in_specs=[tok(d), tok(C_WIDTH)] + _tail_specs(d, d_ff, cond),
        out_specs=tok(d),
        out_shape=jax.ShapeDtypeStruct((b, l, d), F32),
        compiler_params=_params(2),
        name="post_odd",
    )(x, mix, w_out, *_cond_args(cond, 4), wg, wu, wd)


def _head_expand_matrix():
    rows = np.arange(LSE_LANES)[:, None]
    cols = np.arange(A_WIDTH)[None, :]
    return jnp.asarray(rows == cols // HEAD_DIM, BF16)


def kernel(x_prompt, x_sample, c_prompt, c_sample, norm_g, ada_w, ada_b, ffn_w_gate, ffn_w_up, ffn_w_down,
           even_w_in, even_w_out, pool_w, pool_scale, t5_table, odd_w_qkv, odd_w_out, odd_rpb):
    depth = norm_g.shape[0]
    d = x_prompt.shape[-1]
    nb_prompt = c_prompt.shape[0]
    mod_all = _ada(jnp.concatenate([c_prompt, c_sample], axis=0), ada_w, ada_b)
    expand = _head_expand_matrix()
    biases = [_dilated_bias(t5_table, dil) for dil in A_DILATIONS]

    layers = []
    for layer in range(depth):
        i = layer // 2
        w = dict(ffn=(ffn_w_gate[layer].astype(BF16), ffn_w_up[layer].astype(BF16), ffn_w_down[layer].astype(BF16)))
        if layer % 2 == 0:
            wq, wk, wv, wu = jnp.split(even_w_in[i], [A_WIDTH, 2 * A_WIDTH, 3 * A_WIDTH], axis=-1)
            w.update(w_in=jnp.concatenate([wk, wv, wq * Q_SCALE, wu], axis=-1).astype(BF16),
                     w_out=even_w_out[i].astype(BF16), pool_w=pool_w[i].astype(BF16), pool_scale=pool_scale[i])
        else:
            wq, wk, wv = jnp.split(odd_w_qkv[i], 3, axis=-1)
            w.update(w_in=jnp.concatenate([wq * Q_SCALE, wk, wv], axis=-1).astype(BF16),
                     w_out=odd_w_out[i].astype(BF16), rpb=_natten_bias(odd_rpb[i]))
        layers.append(w)

    mod4 = mod_all.reshape(depth, mod_all.shape[1], 1, 6 * d)

    def trunk(x, first_row, tm):
        b, l, _ = x.shape
        for layer, w in enumerate(layers):
            cond = (norm_g, mod4, layer, first_row)
            if layer % 2 == 0:
                z1, z4, z16 = _pre_even(x, cond, w["w_in"], tm)
                branches = [_dilated_branch(zd, bias)
                            for zd, bias in zip((z1.reshape(b, 1, l, EVEN_IN), z4, z16), biases)]
                x = _post_even(x, z1, branches, expand, w["pool_w"], w["pool_scale"], w["w_out"], cond,
                               *w["ffn"], tm)
            else:
                z = _pre_odd(x, cond, w["w_in"], tm)
                mix = _natten(z, w["rpb"])
                x = _post_odd(x, mix, w["w_out"], cond, *w["ffn"], tm)
        return x

    y_prompt = trunk(x_prompt, 0, TOKEN_TILE)
    y_sample = trunk(x_sample, nb_prompt, TOKEN_TILE)
    return (y_prompt, y_sample)
```

```python
import functools

import numpy as np
import jax
import jax.numpy as jnp
from jax import lax
from jax.experimental import pallas as pl
from jax.experimental.pallas import tpu as pltpu

D_MODEL = 1024
HEAD_DIM = 64
ATTN_SCALE = HEAD_DIM ** -0.5
LOG2E = float(np.log2(np.e))
LN2 = float(np.log(2.0))
Q_SCALE = ATTN_SCALE * LOG2E
A_HEADS = 8
A_WIDTH = A_HEADS * HEAD_DIM
A_DILATIONS = (1, 4, 16)
A_BLOCK = 128
A_RADIUS = 64
A_CHUNK = 1024
A_LOOKAHEAD = 4
B_GROUPS = 4
B_WINDOWS = (2, 4, 8, 16)
B_WIDTH = 512
B_GROUP_DIM = 128
B_HALO = 16
B_PAD = 64
EVEN_IN = 3 * A_WIDTH + B_WIDTH
C_HEADS = 16
C_WIDTH = C_HEADS * HEAD_DIM
GRID_W = 64
NA_ROWS = 8
NA_COLS = 16
NA_CHUNK_ROWS = 16
NA_QUAD = 4
NA_ROWOFFS = 2 * NA_ROWS - 1
NA_LOOKAHEAD = 2
T5_BUCKETS = 32
T5_MAX_DIST = 1024
NEG = -1e30
EPS = 1e-6
LANES = 128
LSE_LANES = 128
TOKEN_TILE = 512
TAIL_SUBTILES = 2
PRE_SUBTILES = 2
PRE_COL_CHUNK = 1024
FFN_CHUNK = 1536
V7X_VMEM_LIMIT_BYTES = 60 * 1024 * 1024

BF16 = jnp.bfloat16
F32 = jnp.float32


def _const_spec(shape):
    zeros = (0,) * len(shape)
    return pl.BlockSpec(shape, lambda *_: zeros, pipeline_mode=pl.Buffered(1))


def _params(n_axes):
    return pltpu.CompilerParams(dimension_semantics=("arbitrary",) * n_axes,
                                vmem_limit_bytes=V7X_VMEM_LIMIT_BYTES)


def _rms(x, g):
    return x * lax.rsqrt(jnp.mean(x * x, axis=-1, keepdims=True) + EPS) * g


def _ada_kernel(c_ref, w_ref, b_ref, o_ref):
    c = c_ref[...]
    s = (c * jax.nn.sigmoid(c)).astype(BF16)
    o_ref[...] = jnp.dot(s, w_ref[...].astype(BF16), preferred_element_type=F32) + b_ref[...]


def _ada(c_all, ada_w, ada_b):
    depth, d, six_d = ada_w.shape
    nb = c_all.shape[0]
    return pl.pallas_call(
        _ada_kernel,
        grid=(depth, six_d // d),
        in_specs=[pl.BlockSpec((nb, d), lambda l, j: (0, 0)),
                  pl.BlockSpec((None, d, d), lambda l, j: (l, 0, j)),
                  pl.BlockSpec((None, 1, d), lambda l, j: (l, 0, j))],
        out_specs=pl.BlockSpec((None, nb, d), lambda l, j: (l, 0, j)),
        out_shape=jax.ShapeDtypeStruct((depth, nb, six_d), F32),
        compiler_params=_params(2),
        name="ada_mod",
    )(c_all, ada_w, ada_b.reshape(depth, 1, six_d))


def _modulated(x_ref, g_ref, sh_ref, sc_ref):
    return (_rms(x_ref[...], g_ref[0:1]) * (1.0 + sc_ref[...]) + sh_ref[...]).astype(BF16)


def _pre_odd_kernel(x_ref, g_ref, sh_ref, sc_ref, w_ref, o_ref, *, col_chunk):
    hb = _modulated(x_ref, g_ref, sh_ref, sc_ref)
    for j in range(w_ref.shape[1] // col_chunk):
        cols = slice(j * col_chunk, (j + 1) * col_chunk)
        o_ref[:, cols] = jnp.dot(hb, w_ref[:, cols], preferred_element_type=F32).astype(BF16)


def _pre_even_kernel(x_ref, g_ref, sh_ref, sc_ref, w_ref, perm_ref, z1_ref, z4_ref, z16_ref, *, col_chunk, tm):
    rows = tm // PRE_SUBTILES
    attn_w = 3 * A_WIDTH
    for sub in range(PRE_SUBTILES):
        r = slice(sub * rows, (sub + 1) * rows)
        hb = (_rms(x_ref[r, :], g_ref[0:1]) * (1.0 + sc_ref[...]) + sh_ref[...]).astype(BF16)
        for j in range(w_ref.shape[1] // col_chunk):
            cols = slice(j * col_chunk, (j + 1) * col_chunk)
            zb = jnp.dot(hb, w_ref[:, cols], preferred_element_type=F32).astype(BF16)
            z1_ref[r, cols] = zb
            width = min(col_chunk, attn_w - j * col_chunk)
            if width <= 0:
                continue
            out_cols = slice(j * col_chunk, j * col_chunk + width)
            for k, (dil, ref) in enumerate(((4, z4_ref), (16, z16_ref))):
                n = rows // dil
                zp = jnp.dot(perm_ref[k], zb[:, :width], preferred_element_type=F32).astype(BF16)
                ref[:, sub * n:(sub + 1) * n, out_cols] = zp.reshape(dil, n, width)


def _cond_specs(d, cond, chunks):
    _, _, layer, first = cond
    vec = lambda j: pl.BlockSpec((None, None, 1, d), lambda i, t: (layer, first + i, 0, j))
    return [pl.BlockSpec((None, 4, d), lambda i, t: (layer, 0, 0))] + [vec(j) for j in chunks]


def _cond_args(cond, n_chunks):
    return (cond[0],) + (cond[1],) * n_chunks


def _pre_specs(d, n, tm, cond):
    return ([pl.BlockSpec((None, tm, d), lambda i, t: (i, t, 0))] + _cond_specs(d, cond, (0, 1))
            + [_const_spec((d, n))])


def _pre_odd(x, cond, w, tm):
    b, l, d = x.shape
    n = w.shape[1]
    return pl.pallas_call(
        functools.partial(_pre_odd_kernel, col_chunk=PRE_COL_CHUNK),
        grid=(b, l // tm),
        in_specs=_pre_specs(d, n, tm, cond),
        out_specs=pl.BlockSpec((None, tm, n), lambda i, t: (i, t, 0)),
        out_shape=jax.ShapeDtypeStruct((b, l, n), BF16),
        compiler_params=_params(2),
        name="pre_odd",
    )(x, *_cond_args(cond, 2), w)


def _pre_even(x, cond, w, tm):
    b, l, d = x.shape
    n = w.shape[1]
    qkv = 3 * A_WIDTH
    rows = tm // PRE_SUBTILES
    return pl.pallas_call(
        functools.partial(_pre_even_kernel, col_chunk=PRE_COL_CHUNK, tm=tm),
        grid=(b, l // tm),
        in_specs=_pre_specs(d, n, tm, cond) + [_const_spec((2, rows, rows))],
        out_specs=[pl.BlockSpec((None, tm, n), lambda i, t: (i, t, 0)),
                   pl.BlockSpec((None, 4, tm // 4, qkv), lambda i, t: (i, 0, t, 0)),
                   pl.BlockSpec((None, 16, tm // 16, qkv), lambda i, t: (i, 0, t, 0))],
        out_shape=[jax.ShapeDtypeStruct((b, l, n), BF16),
                   jax.ShapeDtypeStruct((b, 4, l // 4, qkv), BF16),
                   jax.ShapeDtypeStruct((b, 16, l // 16, qkv), BF16)],
        compiler_params=_params(2),
        name="pre_even",
    )(x, *_cond_args(cond, 2), w, _residue_permutations(rows, (4, 16)))


def _residue_permutations(rows, dilations):
    out = np.arange(rows)[:, None]
    src = np.arange(rows)[None, :]
    return jnp.asarray(np.stack([src == dil * (out % (rows // dil)) + out // (rows // dil) for dil in dilations]),
                       BF16)


def _t5_bucket(rel):
    nb = T5_BUCKETS // 2
    max_exact = nb // 2
    ret = (rel > 0).astype(np.int32) * nb
    n = np.abs(rel)
    large = max_exact + (np.log(np.maximum(n, 1) / max_exact) / np.log(T5_MAX_DIST / max_exact)
                         * (nb - max_exact)).astype(np.int32)
    large = np.minimum(large, nb - 1)
    return (ret + np.where(n < max_exact, n, large)).astype(np.int32)


def _lookup_last(table, idx):
    n = table.shape[-1]
    onehot = jnp.asarray(idx)[None] == jnp.arange(n, dtype=jnp.int32).reshape((n,) + (1,) * idx.ndim)
    expanded = table.astype(F32).reshape(table.shape + (1,) * idx.ndim)
    return jnp.sum(jnp.where(onehot, expanded, 0.0), axis=table.ndim - 1)


def _dilated_bias(t5_table, dilation):
    kw = A_BLOCK + 2 * A_RADIUS
    rel = np.arange(kw)[:, None] - A_RADIUS - np.arange(A_BLOCK)[None, :]
    bias = _lookup_last(jnp.transpose(t5_table), _t5_bucket(rel * dilation))
    band = np.abs(rel) <= A_RADIUS
    variants = []
    for v in range(4):
        row_ok = np.ones((kw, 1), bool)
        if v & 1:
            row_ok[:A_RADIUS] = False
        if v & 2:
            row_ok[kw - A_RADIUS:] = False
        masked = jnp.where((band & row_ok)[None], bias * LOG2E, NEG)
        variants.append(masked.reshape(A_HEADS // 2, 2, kw, A_BLOCK).transpose(0, 2, 1, 3)
                        .reshape(A_HEADS // 2, kw, 2 * A_BLOCK))
    return jnp.stack(variants, axis=0)


def _dilated_kernel(q_ref, kvp_ref, kvc_ref, kvn_ref, bias_ref, o_ref, lse_ref, kv_buf, vt_buf, *, chunk, sub_len, group):
    kw = A_BLOCK + 2 * A_RADIUS
    pair_w = 2 * HEAD_DIM
    n_pairs = A_HEADS // 2
    kv_buf[:, 0:A_RADIUS] = kvp_ref[...]
    kv_buf[:, A_RADIUS:A_RADIUS + chunk] = kvc_ref[...]
    kv_buf[:, A_RADIUS + chunk:] = kvn_ref[...]
    for rl in range(group):
        vt_buf[rl] = kv_buf[rl, :, A_WIDTH:2 * A_WIDTH].T
    chunk_start = pl.program_id(2) * chunk
    first_head = lax.broadcasted_iota(jnp.int32, (A_BLOCK, pair_w), 1) < HEAD_DIM
    items = [(rl, j, jp) for rl in range(group) for j in range(chunk // A_BLOCK) for jp in range(n_pairs)]

    def scores(rl, j, jp):
        row0 = j * A_BLOCK
        pos = chunk_start + row0
        variant = (pos == 0).astype(jnp.int32) + 2 * (pos + A_BLOCK == sub_len).astype(jnp.int32)
        cols = slice(jp * pair_w, (jp + 1) * pair_w)
        qp = q_ref[rl, row0:row0 + A_BLOCK, cols].astype(F32)
        qbd = jnp.concatenate([jnp.where(first_head, qp, 0.0), jnp.where(first_head, 0.0, qp)],
                              axis=0).astype(BF16)
        kp = kv_buf[rl, row0:row0 + kw, cols]
        s = lax.dot_general(kp, qbd, (((1,), (1,)), ((), ())), preferred_element_type=F32)
        return s + bias_ref[variant, jp]

    pending = [scores(*item) for item in items[:A_LOOKAHEAD]]
    lse_rows = []
    for idx, (rl, j, jp) in enumerate(items):
        row0 = j * A_BLOCK
        cols = slice(jp * pair_w, (jp + 1) * pair_w)
        v_t = vt_buf[rl, jp * pair_w:(jp + 1) * pair_w, row0:row0 + kw]
        s = pending.pop(0)
        if idx + A_LOOKAHEAD < len(items):
            pending.append(scores(*items[idx + A_LOOKAHEAD]))
        m = jnp.max(s, axis=0, keepdims=True)
        p = jnp.exp2(s - m)
        den = jnp.sum(p, axis=0, keepdims=True)
        res = jnp.dot(v_t, p.astype(BF16), preferred_element_type=F32)
        res = res * (1.0 / den)
        o_t = jnp.concatenate([res[0:HEAD_DIM, 0:A_BLOCK], res[HEAD_DIM:pair_w, A_BLOCK:2 * A_BLOCK]], axis=0)
        o_ref[rl, row0:row0 + A_BLOCK, cols] = o_t.T.astype(BF16)
        lse = (m + jnp.log2(den)) * LN2
        lse_rows += [lse[:, 0:A_BLOCK], lse[:, A_BLOCK:2 * A_BLOCK]]
        if jp == n_pairs - 1:
            lse_t = jnp.concatenate(lse_rows + [jnp.zeros((LSE_LANES - A_HEADS, A_BLOCK), F32)], axis=0)
            lse_ref[rl, row0:row0 + A_BLOCK, :] = lse_t.T
            lse_rows = []


def _dilated_branch(zd, bias):
    b, dilation, sub_len, _ = zd.shape
    chunk = min(A_CHUNK, sub_len)
    group = min(dilation, A_CHUNK // chunk)
    halo_blocks = sub_len // A_RADIUS
    per_chunk = chunk // A_RADIUS
    kw = A_BLOCK + 2 * A_RADIUS
    kv_w = 2 * A_WIDTH
    return pl.pallas_call(
        functools.partial(_dilated_kernel, chunk=chunk, sub_len=sub_len, group=group),
        grid=(b, dilation // group, sub_len // chunk),
        in_specs=[
            pl.BlockSpec((None, group, chunk, A_WIDTH), lambda i, r, c: (i, r, c, 2)),
            pl.BlockSpec((None, group, A_RADIUS, kv_w),
                         lambda i, r, c: (i, r, jnp.maximum(c * per_chunk - 1, 0), 0)),
            pl.BlockSpec((None, group, chunk, kv_w), lambda i, r, c: (i, r, c, 0)),
            pl.BlockSpec((None, group, A_RADIUS, kv_w),
                         lambda i, r, c: (i, r, jnp.minimum((c + 1) * per_chunk, halo_blocks - 1), 0)),
            _const_spec((4, A_HEADS // 2, kw, 2 * A_BLOCK)),
        ],
        out_specs=[pl.BlockSpec((None, group, chunk, A_WIDTH), lambda i, r, c: (i, r, c, 0)),
                   pl.BlockSpec((None, group, chunk, LSE_LANES), lambda i, r, c: (i, r, c, 0))],
        out_shape=[jax.ShapeDtypeStruct((b, dilation, sub_len, A_WIDTH), BF16),
                   jax.ShapeDtypeStruct((b, dilation, sub_len, LSE_LANES), F32)],
        scratch_shapes=[pltpu.VMEM((group, chunk + 2 * A_RADIUS, kv_w), BF16),
                        pltpu.VMEM((group, A_WIDTH, chunk + 2 * A_RADIUS), BF16)],
        compiler_params=_params(3),
        name=f"dilated_d{dilation}",
    )(zd, zd, zd, zd, bias)


def _natten_bias(rpb):
    c = np.arange(GRID_W)
    cstart = np.clip(c - NA_COLS // 2, 0, GRID_W - NA_COLS)
    colmask = (c[None, :] >= cstart[:, None]) & (c[None, :] < cstart[:, None] + NA_COLS)
    coloff = np.clip(c[None, :] - c[:, None] + NA_COLS - 1, 0, 2 * NA_COLS - 2)
    bias = jnp.where(colmask[None, None], _lookup_last(rpb, coloff.astype(np.int32)) * LOG2E, NEG)
    bias = jnp.transpose(bias, (0, 1, 3, 2))
    bias = bias.reshape(C_HEADS // NA_QUAD, NA_QUAD, NA_ROWOFFS, GRID_W, GRID_W)
    return jnp.transpose(bias, (0, 2, 3, 1, 4)).reshape(C_HEADS // NA_QUAD, NA_ROWOFFS, GRID_W, NA_QUAD * GRID_W)


def _natten_kernel(q_ref, kp_ref, kc_ref, kn_ref, vp_ref, vc_ref, vn_ref, bias_ref, o_ref, k_buf, v_buf,
                   *, grid_rows):
    halo = (NA_ROWS // 2) * GRID_W
    chunk = NA_CHUNK_ROWS * GRID_W
    nkeys = NA_ROWS * GRID_W
    quad_w = NA_QUAD * HEAD_DIM
    pair_w = 2 * HEAD_DIM
    k_buf[0:halo] = kp_ref[...]
    k_buf[halo:halo + chunk] = kc_ref[...]
    k_buf[halo + chunk:] = kn_ref[...]
    v_buf[0:halo] = vp_ref[...]
    v_buf[halo:halo + chunk] = vc_ref[...]
    v_buf[halo + chunk:] = vn_ref[...]
    row_base = pl.program_id(1) * NA_CHUNK_ROWS
    lane_head = lax.broadcasted_iota(jnp.int32, (GRID_W, quad_w), 1) // HEAD_DIM
    first_head = lax.broadcasted_iota(jnp.int32, (GRID_W, pair_w), 1) < HEAD_DIM

    def window(rr):
        r = row_base + rr
        rs = jnp.clip(r - NA_ROWS // 2, 0, grid_rows - NA_ROWS)
        return rs - r + (NA_ROWS - 1), pl.multiple_of((rs - (row_base - NA_ROWS // 2)) * GRID_W, GRID_W)

    windows = [window(rr) for rr in range(NA_CHUNK_ROWS)]
    items = [(rr, g) for rr in range(NA_CHUNK_ROWS) for g in range(C_HEADS // NA_QUAD)]

    def scores(rr, g):
        ro0, k0 = windows[rr]
        cols = slice(g * quad_w, (g + 1) * quad_w)
        qs = q_ref[rr * GRID_W:(rr + 1) * GRID_W, cols].astype(F32)
        qbd = jnp.concatenate([jnp.where(lane_head == hh, qs, 0.0) for hh in range(NA_QUAD)],
                              axis=0).astype(BF16)
        kq = k_buf[pl.ds(k0, nkeys), cols]
        s = lax.dot_general(kq, qbd, (((1,), (1,)), ((), ())), preferred_element_type=F32)
        return jnp.concatenate([s[i * GRID_W:(i + 1) * GRID_W] + bias_ref[g, ro0 + i]
                                for i in range(NA_ROWS)], axis=0)

    pending = [scores(*item) for item in items[:NA_LOOKAHEAD]]
    for idx, (rr, g) in enumerate(items):
        cols = slice(g * quad_w, (g + 1) * quad_w)
        vq = v_buf[pl.ds(windows[rr][1], nkeys), cols]
        s = pending.pop(0)
        if idx + NA_LOOKAHEAD < len(items):
            pending.append(scores(*items[idx + NA_LOOKAHEAD]))
        m = jnp.max(s, axis=0, keepdims=True)
        p = jnp.exp2(s - m)
        den = jnp.sum(p, axis=0, keepdims=True)
        res = lax.dot_general(vq, p.astype(BF16), (((0,), (0,)), ((), ())), preferred_element_type=F32)
        res = res * (1.0 / den)
        slabs = []
        for half in range(NA_QUAD // 2):
            blk = res[half * pair_w:(half + 1) * pair_w, half * pair_w:(half + 1) * pair_w].T
            slabs.append(jnp.where(first_head, blk[0:GRID_W], blk[GRID_W:2 * GRID_W]))
        o_ref[rr * GRID_W:(rr + 1) * GRID_W, cols] = jnp.concatenate(slabs, axis=-1).astype(BF16)


def _natten(z, bias):
    b, l, _ = z.shape
    grid_rows = l // GRID_W
    chunk = NA_CHUNK_ROWS * GRID_W
    halo = (NA_ROWS // 2) * GRID_W
    halo_blocks = l // halo
    per_chunk = chunk // halo

    def cur(col):
        return pl.BlockSpec((None, chunk, C_WIDTH), lambda i, c: (i, c, col))

    def prev(col):
        return pl.BlockSpec((None, halo, C_WIDTH), lambda i, c: (i, jnp.maximum(c * per_chunk - 1, 0), col))

    def nxt(col):
        return pl.BlockSpec((None, halo, C_WIDTH),
                            lambda i, c: (i, jnp.minimum((c + 1) * per_chunk, halo_blocks - 1), col))

    return pl.pallas_call(
        functools.partial(_natten_kernel, grid_rows=grid_rows),
        grid=(b, l // chunk),
        in_specs=[cur(0), prev(1), cur(1), nxt(1), prev(2), cur(2), nxt(2),
                  _const_spec((C_HEADS // NA_QUAD, NA_ROWOFFS, GRID_W, NA_QUAD * GRID_W))],
        out_specs=pl.BlockSpec((None, chunk, C_WIDTH), lambda i, c: (i, c, 0)),
        out_shape=jax.ShapeDtypeStruct((b, l, C_WIDTH), BF16),
        scratch_shapes=[pltpu.VMEM((chunk + 2 * halo, C_WIDTH), BF16),
                        pltpu.VMEM((chunk + 2 * halo, C_WIDTH), BF16)],
        compiler_params=_params(2),
        name="natten",
    )(z, z, z, z, z, z, z, bias)


def _ffn_tail(x_ref, mix_of_group, wo_ref, g_ref, gate1_ref, sh2_ref, sc2_ref, gate2_ref, wg_ref, wu_ref, wd_ref,
              o_ref, side_work=()):
    rows = o_ref.shape[0] // TAIL_SUBTILES
    groups = [slice(i * rows, (i + 1) * rows) for i in range(TAIL_SUBTILES)]
    d_ff = wg_ref.shape[1]
    bounds = [min(c * FFN_CHUNK, d_ff) for c in range(-(-d_ff // FFN_CHUNK) + 1)]
    chunks = [slice(lo, hi) for lo, hi in zip(bounds[:-1], bounds[1:])]
    side_work = list(side_work)

    def emit_side_work():
        if side_work:
            side_work.pop(0)()

    mixes = [mix_of_group(i) for i in range(TAIL_SUBTILES)]
    ys = [jnp.dot(mix, wo_ref[...], preferred_element_type=F32) for mix in mixes]
    emit_side_work()
    x1s = [x_ref[r, :] + gate1_ref[...] * _rms(y, g_ref[1:2]) for r, y in zip(groups, ys)]
    hs = [(_rms(x1, g_ref[2:3]) * (1.0 + sc2_ref[...]) + sh2_ref[...]).astype(BF16) for x1 in x1s]
    y2s = None
    for cols in chunks:
        gates = [jnp.dot(h, wg_ref[:, cols], preferred_element_type=F32) for h in hs]
        ups = [jnp.dot(h, wu_ref[:, cols], preferred_element_type=F32) for h in hs]
        emit_side_work()
        acts = [(gate * jax.nn.sigmoid(gate) * up).astype(BF16) for gate, up in zip(gates, ups)]
        parts = [jnp.dot(act, wd_ref[cols, :], preferred_element_type=F32) for act in acts]
        emit_side_work()
        y2s = parts if y2s is None else [y2 + part for y2, part in zip(y2s, parts)]
    while side_work:
        emit_side_work()
    for r, x1, y2 in zip(groups, x1s, y2s):
        o_ref[r, :] = x1 + gate2_ref[...] * _rms(y2, g_ref[3:4])


def _token_order(src_ref, buf, dilation, rows, group):
    n_slabs = buf.shape[0]
    n = rows // dilation
    for rho in range(dilation):
        part = src_ref[rho, group * n:(group + 1) * n, :].astype(F32)
        for cb in range(n_slabs):
            buf[cb, pl.ds(group * rows + rho, n, stride=dilation), :] = part[:, cb * LANES:(cb + 1) * LANES]
    return jnp.concatenate([buf[cb, group * rows:(group + 1) * rows, :] for cb in range(n_slabs)], axis=-1)


def _post_even_kernel(x_ref, o1_ref, o4_ref, o16_ref, l1_ref, l4_ref, l16_ref, up_ref, uc_ref, un_ref,
                      expand_ref, unperm_ref, band_ref, pw_ref, ps_ref, wo_ref, g_ref, gate1_ref, sh2_ref, sc2_ref,
                      gate2_ref, wg_ref, wu_ref, wd_ref, o_ref, mix_buf, u_buf, l4_buf, l16_buf,
                      *, tm, seq_len, tiles_per_seq, n_tiles):
    rows = tm // TAIL_SUBTILES
    step = pl.program_id(0)
    write_slot = step % 2
    read_slot = 1 - write_slot

    @pl.when(step == 0)
    def _():
        mix_buf[1] = jnp.zeros(mix_buf.shape[1:], mix_buf.dtype)

    t0 = (jnp.minimum(step, n_tiles - 1) % tiles_per_seq) * tm
    zeros = jnp.zeros((B_PAD - B_HALO, B_WIDTH), BF16)
    u_buf[0:B_PAD - B_HALO] = zeros
    u_buf[B_PAD - B_HALO:B_PAD] = jnp.where(t0 > 0, up_ref[...], jnp.zeros_like(up_ref))
    u_buf[B_PAD:B_PAD + tm] = uc_ref[...]
    u_buf[B_PAD + tm:B_PAD + tm + B_HALO] = jnp.where(t0 + tm < seq_len, un_ref[...], jnp.zeros_like(un_ref))
    u_buf[B_PAD + tm + B_HALO:] = zeros

    state = [dict() for _ in range(TAIL_SUBTILES)]

    def token_order_outputs():
        for i, st in enumerate(state):
            st["outs"] = [o1_ref[i * rows:(i + 1) * rows, :].astype(F32)]
            for k, (dil, ref) in enumerate(((4, o4_ref), (16, o16_ref))):
                n = rows // dil
                grouped = ref[:, i * n:(i + 1) * n, :].reshape(rows, A_WIDTH)
                st["outs"].append(jnp.dot(unperm_ref[k], grouped, preferred_element_type=F32))

    def combine_branches():
        for i, st in enumerate(state):
            l1, l2, l3 = (l1_ref[i * rows:(i + 1) * rows, :], _token_order(l4_ref, l4_buf, 4, rows, i),
                          _token_order(l16_ref, l16_buf, 16, rows, i))
            mx = jnp.maximum(jnp.maximum(l1, l2), l3)
            es = (jnp.exp(l1 - mx), jnp.exp(l2 - mx), jnp.exp(l3 - mx))
            inv = 1.0 / (es[0] + es[1] + es[2])
            ya = jnp.zeros((rows, A_WIDTH), F32)
            for e, o_br in zip(es, st["outs"]):
                ya = ya + jnp.dot((e * inv).astype(BF16), expand_ref[...], preferred_element_type=F32) * o_br
            st["ya"] = ya.astype(BF16)

    def pool():
        for i, st in enumerate(state):
            t = t0 + i * rows + lax.broadcasted_iota(jnp.int32, (rows, 1), 0)
            st["pooled"] = []
            for g, w in enumerate(B_WINDOWS):
                cols = slice(g * B_GROUP_DIM, (g + 1) * B_GROUP_DIM)
                total = jnp.dot(band_ref[g], u_buf[i * rows:i * rows + rows + 2 * B_PAD, cols],
                                preferred_element_type=F32)
                cnt = (jnp.clip(t + w // 2, 0, seq_len) - jnp.clip(t - w // 2, 0, seq_len)).astype(F32)
                center = u_buf[B_PAD + i * rows:B_PAD + (i + 1) * rows, cols].astype(F32)
                st["pooled"].append((total * (1.0 / cnt) - center).astype(BF16))

    def store_mix():
        for i, st in enumerate(state):
            yb = [(jnp.dot(pooled, pw_ref[g], preferred_element_type=F32)
                   * ps_ref[:, g * B_GROUP_DIM:(g + 1) * B_GROUP_DIM]).astype(BF16)
                  for g, pooled in enumerate(st["pooled"])]
            mix_buf[write_slot, i * rows:(i + 1) * rows, :] = jnp.concatenate([st["ya"]] + yb, axis=-1)

    mix_of_group = lambda i: mix_buf[read_slot, i * rows:(i + 1) * rows, :]
    _ffn_tail(x_ref, mix_of_group, wo_ref, g_ref, gate1_ref, sh2_ref, sc2_ref, gate2_ref, wg_ref, wu_ref, wd_ref,
              o_ref, side_work=[token_order_outputs, combine_branches, pool, store_mix])


def _post_odd_kernel(x_ref, mix_ref, wo_ref, g_ref, gate1_ref, sh2_ref, sc2_ref, gate2_ref,
                     wg_ref, wu_ref, wd_ref, o_ref):
    rows = o_ref.shape[0] // TAIL_SUBTILES
    mix_of_group = lambda i: mix_ref[i * rows:(i + 1) * rows, :]
    _ffn_tail(x_ref, mix_of_group, wo_ref, g_ref, gate1_ref, sh2_ref, sc2_ref, gate2_ref, wg_ref, wu_ref, wd_ref, o_ref)


def _tail_specs(d, d_ff, cond):
    return ([_const_spec((d, d))] + _cond_specs(d, cond, (2, 3, 4, 5))
            + [_const_spec((d, d_ff)), _const_spec((d, d_ff)), _const_spec((d_ff, d))])


def _pool_bands(rows):
    offset = np.arange(rows + 2 * B_PAD)[None, :] - B_PAD - np.arange(rows)[:, None]
    return jnp.asarray(np.stack([(offset >= -(w // 2)) & (offset < w // 2) for w in B_WINDOWS]), BF16)


def _post_even(x, z1, branches, expand, pool_w, pool_scale, w_out, cond, wg, wu, wd, tm):
    b, l, d = x.shape
    d_ff = wg.shape[1]
    (o1, l1), (o4, l4), (o16, l16) = branches
    nt = l // tm
    n_tiles = b * nt
    halo_blocks = l // B_HALO
    per_tile = tm // B_HALO
    rows = tm // TAIL_SUBTILES
    u_col = 3
    norm_g, mod, layer, first = cond

    def tail_tile(i):
        f = jnp.maximum(i - 1, 0)
        return f // nt, f % nt

    def mix_tile(i):
        m = jnp.minimum(i, n_tiles - 1)
        return m // nt, m % nt

    tail_tok = pl.BlockSpec((None, tm, d), lambda i: (*tail_tile(i), 0))
    tail_vec = lambda j: pl.BlockSpec((None, None, 1, d), lambda i: (layer, first + tail_tile(i)[0], 0, j))
    tok = lambda w: pl.BlockSpec((None, tm, w), lambda i: (*mix_tile(i), 0))
    grouped = lambda dil, w: pl.BlockSpec((None, dil, tm // dil, w),
                                          lambda i: (mix_tile(i)[0], 0, mix_tile(i)[1], 0))
    in_specs = [tail_tok, tok(A_WIDTH), grouped(4, A_WIDTH), grouped(16, A_WIDTH),
                tok(LSE_LANES), grouped(4, LSE_LANES), grouped(16, LSE_LANES),
                pl.BlockSpec((None, B_HALO, B_WIDTH),
                             lambda i: (mix_tile(i)[0], jnp.maximum(mix_tile(i)[1] * per_tile - 1, 0), u_col)),
                pl.BlockSpec((None, tm, B_WIDTH), lambda i: (*mix_tile(i), u_col)),
                pl.BlockSpec((None, B_HALO, B_WIDTH),
                             lambda i: (mix_tile(i)[0],
                                        jnp.minimum((mix_tile(i)[1] + 1) * per_tile, halo_blocks - 1), u_col)),
                _const_spec((LSE_LANES, A_WIDTH)), _const_spec((2, rows, rows)),
                _const_spec((B_GROUPS, rows, rows + 2 * B_PAD)),
                _const_spec((B_GROUPS, B_GROUP_DIM, B_GROUP_DIM)), _const_spec((1, B_WIDTH)),
                _const_spec((d, d)), pl.BlockSpec((None, 4, d), lambda i: (layer, 0, 0)),
                tail_vec(2), tail_vec(3), tail_vec(4), tail_vec(5),
                _const_spec((d, d_ff)), _const_spec((d, d_ff)), _const_spec((d_ff, d))]
    return pl.pallas_call(
        functools.partial(_post_even_kernel, tm=tm, seq_len=l, tiles_per_seq=nt, n_tiles=n_tiles),
        grid=(n_tiles + 1,),
        in_specs=in_specs,
        out_specs=tail_tok,
        out_shape=jax.ShapeDtypeStruct((b, l, d), F32),
        scratch_shapes=[pltpu.VMEM((2, tm, A_WIDTH + B_WIDTH), BF16),
                        pltpu.VMEM((tm + 2 * B_PAD, B_WIDTH), BF16),
                        pltpu.VMEM((LSE_LANES // LANES, tm, LANES), F32),
                        pltpu.VMEM((LSE_LANES // LANES, tm, LANES), F32)],
        compiler_params=_params(1),
        name="post_even",
    )(x, o1.reshape(b, l, A_WIDTH), o4, o16, l1.reshape(b, l, LSE_LANES), l4, l16, z1, z1, z1,
      expand, jnp.swapaxes(_residue_permutations(rows, (4, 16)), 1, 2), _pool_bands(rows), pool_w,
      pool_scale.reshape(1, B_WIDTH), w_out, norm_g, mod, mod, mod, mod, wg, wu, wd)


def _post_odd(x, mix, w_out, cond, wg, wu, wd, tm):
    b, l, d = x.shape
    d_ff = wg.shape[1]
    tok = lambda w: pl.BlockSpec((None, tm, w), lambda i, t: (i, t, 0))
    return pl.pallas_call(
        _post_odd_kernel,
        grid=(b, l // tm),
        in_specs=[tok(d), tok(C_WIDTH)] + _tail_specs(d, d_ff, cond),
        out_specs=tok(d),
        out_shape=jax.ShapeDtypeStruct((b, l, d), F32),
        compiler_params=_params(2),
        name="post_odd",
    )(x, mix, w_out, *_cond_args(cond, 4), wg, wu, wd)


def _head_expand_matrix():
    rows = np.arange(LSE_LANES)[:, None]
    cols = np.arange(A_WIDTH)[None, :]
    return jnp.asarray(rows == cols // HEAD_DIM, BF16)


def kernel(x_prompt, x_sample, c_prompt, c_sample, norm_g, ada_w, ada_b, ffn_w_gate, ffn_w_up, ffn_w_down,
           even_w_in, even_w_out, pool_w, pool_scale, t5_table, odd_w_qkv, odd_w_out, odd_rpb):
    depth = norm_g.shape[0]
    d = x_prompt.shape[-1]
    nb_prompt = c_prompt.shape[0]
    mod_all = _ada(jnp.concatenate([c_prompt, c_sample], axis=0), ada_w, ada_b)
    expand = _head_expand_matrix()
    biases = [_dilated_bias(t5_table, dil) for dil in A_DILATIONS]

    layers = []
    for layer in range(depth):
        i = layer // 2
        w = dict(ffn=(ffn_w_gate[layer].astype(BF16), ffn_w_up[layer].astype(BF16), ffn_w_down[layer].astype(BF16)))
        if layer % 2 == 0:
            wq, wk, wv, wu = jnp.split(even_w_in[i], [A_WIDTH, 2 * A_WIDTH, 3 * A_WIDTH], axis=-1)
            w.update(w_in=jnp.concatenate([wk, wv, wq * Q_SCALE, wu], axis=-1).astype(BF16),
                     w_out=even_w_out[i].astype(BF16), pool_w=pool_w[i].astype(BF16), pool_scale=pool_scale[i])
        else:
            wq, wk, wv = jnp.split(odd_w_qkv[i], 3, axis=-1)
            w.update(w_in=jnp.concatenate([wq * Q_SCALE, wk, wv], axis=-1).astype(BF16),
                     w_out=odd_w_out[i].astype(BF16), rpb=_natten_bias(odd_rpb[i]))
        layers.append(w)

    mod4 = mod_all.reshape(depth, mod_all.shape[1], 1, 6 * d)

    def trunk(x, first_row, tm):
        b, l, _ = x.shape
        for layer, w in enumerate(layers):
            cond = (norm_g, mod4, layer, first_row)
            if layer % 2 == 0:
                z1, z4, z16 = _pre_even(x, cond, w["w_in"], tm)
                branches = [_dilated_branch(zd, bias)
                            for zd, bias in zip((z1.reshape(b, 1, l, EVEN_IN), z4, z16), biases)]
                x = _post_even(x, z1, branches, expand, w["pool_w"], w["pool_scale"], w["w_out"], cond,
                               *w["ffn"], tm)
            else:
                z = _pre_odd(x, cond, w["w_in"], tm)
                mix = _natten(z, w["rpb"])
                x = _post_odd(x, mix, w["w_out"], cond, *w["ffn"], tm)
        return x

    y_prompt = trunk(x_prompt, 0, TOKEN_TILE)
    y_sample = trunk(x_sample, nb_prompt, TOKEN_TILE)
    return (y_prompt, y_sample)
```
